```python
import jax, jax.numpy as jnp
from jax import lax
import numpy as np

D_MODEL = 1024
BATCH = 4
SEQ = 8192
DEPTH = 2
DEC_BATCH = 128
DEC_SEQ = 4
PAST_LEN = 16384
PAGE_SIZE = 128

N_A_LAYERS = DEPTH // 2
N_B_LAYERS = DEPTH - N_A_LAYERS
CHUNK = 128
GM_HALF = 2 * D_MODEL
GM_GROUPS = 8
GM_GROUP_DIM = GM_HALF // GM_GROUPS
HEAD_DIM = 64
N_HEADS = D_MODEL // HEAD_DIM
N_KV_HEADS = N_HEADS // 8
Q_PER_KV = N_HEADS // N_KV_HEADS
WINDOW = 128
ROT_DIM = HEAD_DIM // 4
ROPE_THETA = 500000.0
D_FF = 2816
CONV_W = 3
PLE_DIM = 256
EPS = 1e-6
NEG_INF = -1e30

kernel_name = "yoco_gmlp_swa_sink_convffn_step"


def rmsnorm(x, w):
    xf = x.astype(jnp.float32)
    y = xf * lax.rsqrt(jnp.mean(xf * xf, axis=-1, keepdims=True) + EPS)
    return (y * w.astype(jnp.float32)).astype(x.dtype)


def partial_rope(x, pos):
    half = ROT_DIM // 2
    inv_freq = ROPE_THETA ** (-jnp.arange(half, dtype=jnp.float32) / half)
    ang = pos.astype(jnp.float32)[:, None] * inv_freq[None, :]
    cos = jnp.cos(ang)[None, :, None, :].astype(x.dtype)
    sin = jnp.sin(ang)[None, :, None, :].astype(x.dtype)
    x1 = x[..., :half]
    x2 = x[..., half:ROT_DIM]
    return jnp.concatenate([x1 * cos - x2 * sin, x2 * cos + x1 * sin, x[..., ROT_DIM:]], axis=-1)


def chunk_gmlp(h, w_in, v_norm, w_s, b_s, w_out, chunk_len):
    B, T, _ = h.shape
    z = jax.nn.gelu(h @ w_in)
    u, v = z[..., :GM_HALF], z[..., GM_HALF:]
    v = rmsnorm(v, v_norm)
    n = T // chunk_len
    L = chunk_len
    causal = jnp.tril(jnp.ones((L, L), dtype=bool))
    ws = jnp.where(causal[None], w_s[:, :L, :L], 0).astype(v.dtype)
    vb = v.reshape(B, n, L, GM_GROUPS, GM_GROUP_DIM)
    mixed = jnp.einsum('gts,bnsgc->bntgc', ws, vb) + b_s[:, :L].T[None, None, :, :, None].astype(v.dtype)
    out = u * mixed.reshape(B, T, GM_HALF)
    return out @ w_out, v


def conv_ffn(h, conv_state, w_gate, w_up, conv_w, conv_b, w_down):
    T = h.shape[1]
    g = h @ w_gate
    gp = jnp.concatenate([conv_state.astype(g.dtype), g], axis=1)
    conv = conv_b.astype(g.dtype)
    for k in range(CONV_W):
        conv = conv + gp[:, k:k + T] * conv_w[k]
    act = jax.nn.gelu(conv) * (h @ w_up)
    return act @ w_down, gp[:, T:]


def band_attention(q, k, v, sinks, k_past, v_past):
    B, T = q.shape[0], q.shape[1]
    if k_past is None:
        nb = T // WINDOW
        lq = WINDOW
        pad = jnp.zeros((B, WINDOW, N_KV_HEADS, HEAD_DIM), k.dtype)
        kp = jnp.concatenate([pad, k], axis=1).reshape(B, nb + 1, WINDOW, N_KV_HEADS, HEAD_DIM)
        vp = jnp.concatenate([pad, v], axis=1).reshape(B, nb + 1, WINDOW, N_KV_HEADS, HEAD_DIM)
        kb = jnp.concatenate([kp[:, :-1], kp[:, 1:]], axis=2)
        vb = jnp.concatenate([vp[:, :-1], vp[:, 1:]], axis=2)
        block_ok = (jnp.arange(nb)[:, None, None] > 0) | (jnp.arange(2 * WINDOW)[None, None, :] >= WINDOW)
    else:
        nb = 1
        lq = T
        kb = jnp.concatenate([k_past.astype(k.dtype), k], axis=1)[:, None]
        vb = jnp.concatenate([v_past.astype(v.dtype), v], axis=1)[:, None]
        block_ok = jnp.ones((1, 1, WINDOW + T), dtype=bool)
    lk = WINDOW + lq
    qi = jnp.arange(lq)[:, None]
    sj = jnp.arange(lk)[None, :]
    mask = ((sj > qi) & (sj <= qi + WINDOW))[None] & block_ok
    qb = q.reshape(B, nb, lq, N_KV_HEADS, Q_PER_KV, HEAD_DIM)
    s = jnp.einsum('bnqkgd,bnskd->bnkgqs', qb, kb, preferred_element_type=jnp.float32) * (HEAD_DIM ** -0.5)
    s = jnp.where(mask[None, :, None, None], s, NEG_INF)
    sink = jnp.broadcast_to(sinks.astype(jnp.float32).reshape(1, 1, N_KV_HEADS, Q_PER_KV, 1, 1), s.shape[:-1] + (1,))
    p = jax.nn.softmax(jnp.concatenate([s, sink], axis=-1), axis=-1)[..., :-1].astype(vb.dtype)
    o = jnp.einsum('bnkgqs,bnskd->bnqkgd', p, vb)
    return o.reshape(B, T, N_HEADS * HEAD_DIM)


def setup_inputs(seed: int = 0) -> dict:
    key = jax.random.key(seed)
    ks = jax.random.split(key, 32)

    def nrm(k, shape, scale=1.0):
        return jax.random.normal(k, shape, jnp.float32) * scale

    def gain(k, shape):
        return 1.0 + 0.05 * jax.random.normal(k, shape, jnp.float32)

    kv_dim = N_KV_HEADS * HEAD_DIM
    return {
        "x_prompt": nrm(ks[0], (BATCH, SEQ, D_MODEL)),
        "x_sample": nrm(ks[1], (DEC_BATCH, DEC_SEQ, D_MODEL)),
        "state_ffn_conv": nrm(ks[2], (DEPTH, DEC_BATCH, CONV_W - 1, D_FF)),
        "cache_k_win": nrm(ks[3], (DEC_BATCH, WINDOW, N_KV_HEADS, HEAD_DIM)),
        "cache_v_win": nrm(ks[4], (DEC_BATCH, WINDOW, N_KV_HEADS, HEAD_DIM)),
        "p_prompt": nrm(ks[5], (DEPTH, BATCH, SEQ, PLE_DIM)),
        "p_sample": nrm(ks[6], (DEPTH, DEC_BATCH, DEC_SEQ, PLE_DIM)),
        "norm_mix": gain(ks[7], (DEPTH, D_MODEL)),
        "gm_w_in": nrm(ks[8], (N_A_LAYERS, D_MODEL, 2 * GM_HALF), D_MODEL ** -0.5),
        "gm_v_norm": gain(ks[9], (N_A_LAYERS, GM_HALF)),
        "gm_w_s": nrm(ks[10], (N_A_LAYERS, GM_GROUPS, CHUNK, CHUNK), CHUNK ** -0.5),
        "gm_b_s": 1.0 + 0.1 * nrm(ks[11], (N_A_LAYERS, GM_GROUPS, CHUNK)),
        "gm_w_out": nrm(ks[12], (N_A_LAYERS, GM_HALF, D_MODEL), GM_HALF ** -0.5),
        "kv_norm": gain(ks[13], (D_MODEL,)),
        "w_kv": nrm(ks[14], (D_MODEL, 2 * kv_dim), D_MODEL ** -0.5),
        "w_q": nrm(ks[15], (N_B_LAYERS, D_MODEL, N_HEADS * HEAD_DIM), D_MODEL ** -0.5),
        "attn_sinks": nrm(ks[16], (N_B_LAYERS, N_HEADS), 0.5),
        "w_o": nrm(ks[17], (N_B_LAYERS, N_HEADS * HEAD_DIM, D_MODEL), (N_HEADS * HEAD_DIM) ** -0.5),
        "norm_ffn": gain(ks[18], (DEPTH, D_MODEL)),
        "ffn_w_gate": nrm(ks[19], (DEPTH, D_MODEL, D_FF), D_MODEL ** -0.5),
        "ffn_w_up": nrm(ks[20], (DEPTH, D_MODEL, D_FF), D_MODEL ** -0.5),
        "ffn_conv_w": nrm(ks[21], (DEPTH, CONV_W, D_FF), CONV_W ** -0.5),
        "ffn_conv_b": nrm(ks[22], (DEPTH, D_FF), 0.02),
        "ffn_w_down": nrm(ks[23], (DEPTH, D_FF, D_MODEL), D_FF ** -0.5),
        "ple_norm": gain(ks[24], (DEPTH, D_MODEL)),
        "ple_w_gate": nrm(ks[25], (DEPTH, D_MODEL, D_MODEL), D_MODEL ** -0.5),
        "ple_w_proj": nrm(ks[26], (DEPTH, PLE_DIM, D_MODEL), PLE_DIM ** -0.5),
        "final_norm": gain(ks[27], (D_MODEL,)),
    }


def reference(x_prompt, x_sample, state_ffn_conv, cache_k_win, cache_v_win, p_prompt, p_sample,
              norm_mix, gm_w_in, gm_v_norm, gm_w_s, gm_b_s, gm_w_out, kv_norm, w_kv, w_q, attn_sinks, w_o,
              norm_ffn, ffn_w_gate, ffn_w_up, ffn_conv_w, ffn_conv_b, ffn_w_down,
              ple_norm, ple_w_gate, ple_w_proj, final_norm):
    kv_dim = N_KV_HEADS * HEAD_DIM

    def shared_kv(x, pos):
        B, T, _ = x.shape
        kv = rmsnorm(x, kv_norm) @ w_kv
        k = partial_rope(kv[..., :kv_dim].reshape(B, T, N_KV_HEADS, HEAD_DIM), pos)
        v = kv[..., kv_dim:].reshape(B, T, N_KV_HEADS, HEAD_DIM)
        return k, v

    def trunk(x, p, conv_state, k_past, v_past, pos, chunk_len):
        B, T, _ = x.shape
        gm_v, new_conv = [], []
        k = v = None
        for i in range(DEPTH):
            h = rmsnorm(x, norm_mix[i])
            if i < N_A_LAYERS:
                mix, v_rows = chunk_gmlp(h, gm_w_in[i], gm_v_norm[i], gm_w_s[i], gm_b_s[i], gm_w_out[i], chunk_len)
                gm_v.append(v_rows)
            else:
                j = i - N_A_LAYERS
                q = partial_rope((h @ w_q[j]).reshape(B, T, N_HEADS, HEAD_DIM), pos)
                mix = band_attention(q, k, v, attn_sinks[j], k_past, v_past) @ w_o[j]
            x = x + mix
            h = rmsnorm(x, norm_ffn[i])
            ffn, cs = conv_ffn(h, conv_state[i], ffn_w_gate[i], ffn_w_up[i], ffn_conv_w[i], ffn_conv_b[i], ffn_w_down[i])
            new_conv.append(cs)
            x = x + ffn
            gate = jax.nn.sigmoid(rmsnorm(x, ple_norm[i]) @ ple_w_gate[i])
            x = x + (p[i] @ ple_w_proj[i]) * gate
            if i == N_A_LAYERS - 1:
                k, v = shared_kv(x, pos)
        return rmsnorm(x, final_norm), gm_v, new_conv, k, v

    zero_conv = jnp.zeros((DEPTH, x_prompt.shape[0], CONV_W - 1, D_FF), x_prompt.dtype)
    pos_prompt = jnp.arange(SEQ, dtype=jnp.int32)
    pos_sample = PAST_LEN + jnp.arange(DEC_SEQ, dtype=jnp.int32)

    y_prompt, _, conv_p, k_p, v_p = trunk(x_prompt, p_prompt, zero_conv, None, None, pos_prompt, CHUNK)
    y_sample, gm_s, conv_s, k_s, v_s = trunk(x_sample, p_sample, state_ffn_conv, cache_k_win, cache_v_win,
                                             pos_sample, DEC_SEQ)

    new_gm_v_sample = jnp.stack(gm_s, axis=0)
    new_conv_prompt = jnp.stack(conv_p, axis=0)
    new_conv_sample = jnp.stack(conv_s, axis=0)
    new_k_prompt = k_p[:, -WINDOW:]
    new_v_prompt = v_p[:, -WINDOW:]
    return (y_prompt, y_sample, new_gm_v_sample, new_conv_prompt, new_conv_sample,
            new_k_prompt, new_v_prompt, k_s, v_s)
```

```python
import functools

import jax
import jax.numpy as jnp
from jax import lax
from jax.experimental import pallas as pl
from jax.experimental.pallas import tpu as pltpu

D_MODEL = 1024
GM_HALF = 2 * D_MODEL
GM_GROUPS = 8
GM_GROUP_DIM = GM_HALF // GM_GROUPS
CHUNK = 128
HEAD_DIM = 64
N_HEADS = D_MODEL // HEAD_DIM
N_KV_HEADS = N_HEADS // 8
KV_DIM = N_KV_HEADS * HEAD_DIM
WINDOW = 128
ROT_DIM = HEAD_DIM // 4
ROT_HALF = ROT_DIM // 2
ROPE_THETA = 500000.0
D_FF = 2816
CONV_W = 3
PLE_DIM = 256
EPS = 1e-6
NEG_INF = -1e30
PAST_LEN = 16384

LANES = 128
SUBLANES = 8
MXU_COLS = 256
VMEM_LIMIT_BYTES = 56 * 1024 * 1024
SLABS = D_MODEL // LANES
SLABS_PER_KV = SLABS // N_KV_HEADS
FF_BLOCKS = D_FF // MXU_COLS
PAD_T = SUBLANES

BF16 = jnp.bfloat16
F32 = jnp.float32


def _rms(x, w):
    ms = jnp.sum(x * x, axis=-1, keepdims=True) * (1.0 / x.shape[-1])
    return x * lax.rsqrt(ms + EPS) * w


def _dot(a, b):
    return jnp.dot(a, b, preferred_element_type=F32)


def _dot_nt(a, b):
    return lax.dot_general(a, b, (((1,), (1,)), ((), ())), preferred_element_type=F32)


def _gelu(x):
    return jax.nn.gelu(x, approximate=True)


def _rope(x, table):
    c = table[:, 0:LANES]
    s1 = table[:, LANES:2 * LANES]
    s2 = table[:, 2 * LANES:3 * LANES]
    return x * c + pltpu.roll(x, ROT_HALF, 1) * s1 + pltpu.roll(x, LANES - ROT_HALF, 1) * s2


def _const_spec(shape):
    zeros = (0,) * len(shape)
    return pl.BlockSpec(shape, lambda i: zeros, pipeline_mode=pl.Buffered(1))


def _row_spec(tm, width):
    return pl.BlockSpec((tm, width), lambda i: (i, 0))


_SMEM_SPEC = pl.BlockSpec(memory_space=pltpu.SMEM)


def _params():
    return pltpu.CompilerParams(dimension_semantics=("arbitrary",), vmem_limit_bytes=VMEM_LIMIT_BYTES)


def _gmlp_kernel(*refs, tm, time_major, nt=None):
    if time_major:
        x_ref, nw_ref, win_ref, vn_ref, ws_ref, bs_ref, wout_ref, o_ref, v_ref, v_scr, out_scr = refs
    else:
        x_ref, nw_ref, win_ref, vn_ref, ws_ref, bias_ref, wout_ref, o_ref, v_scr, out_scr = refs

    x = x_ref[...]
    h = _rms(x, nw_ref[...]).astype(BF16)

    ssq = jnp.zeros((tm, 1), F32)
    for g in range(GM_GROUPS):
        sl = slice(g * GM_GROUP_DIM, (g + 1) * GM_GROUP_DIM)
        vg = _gelu(_dot(h, win_ref[:, GM_HALF + g * GM_GROUP_DIM:GM_HALF + (g + 1) * GM_GROUP_DIM]))
        v_scr[:, sl] = vg
        ssq = ssq + jnp.sum(vg * vg, axis=-1, keepdims=True)
    rinv = lax.rsqrt(ssq * (1.0 / GM_HALF) + EPS)

    if not time_major:
        row = lax.broadcasted_iota(jnp.int32, (CHUNK, CHUNK), 0)
        col = lax.broadcasted_iota(jnp.int32, (CHUNK, CHUNK), 1)
        causal = col <= row

    for g in range(GM_GROUPS):
        sl = slice(g * GM_GROUP_DIM, (g + 1) * GM_GROUP_DIM)
        vg = v_scr[:, sl] * rinv * vn_ref[:, sl]
        ug = _gelu(_dot(h, win_ref[:, sl]))
        if time_major:
            v_ref[:, sl] = vg
            nb = tm // nt
            for t in range(nt):
                m = ws_ref[(g * nt + t) * nt] * vg[0:nb]
                for s in range(1, t + 1):
                    m = m + ws_ref[(g * nt + t) * nt + s] * vg[s * nb:(s + 1) * nb]
                m = m + bs_ref[g * nt + t]
                out_scr[t * nb:(t + 1) * nb, sl] = (ug[t * nb:(t + 1) * nb] * m).astype(BF16)
        else:
            vb = vg.astype(BF16)
            wm = jnp.where(causal, ws_ref[g], 0.0).astype(BF16)
            for c in range(tm // CHUNK):
                rs = slice(c * CHUNK, (c + 1) * CHUNK)
                m = _dot(wm, vb[rs]) + bias_ref[:, sl]
                out_scr[rs, sl] = (ug[rs] * m).astype(BF16)

    o_ref[...] = x + _dot(out_scr[...], wout_ref[...])


def _gmlp_prompt(x, nw, w_in, vn, ws, bias, w_out, tm):
    rows = x.shape[0]
    return pl.pallas_call(
        functools.partial(_gmlp_kernel, tm=tm, time_major=False),
        grid=(rows // tm,),
        in_specs=[
            _row_spec(tm, D_MODEL),
            _const_spec((1, D_MODEL)),
            _const_spec((D_MODEL, 2 * GM_HALF)),
            _const_spec((1, GM_HALF)),
            _const_spec((GM_GROUPS, CHUNK, CHUNK)),
            _const_spec((CHUNK, GM_HALF)),
            _const_spec((GM_HALF, D_MODEL)),
        ],
        out_specs=_row_spec(tm, D_MODEL),
        out_shape=jax.ShapeDtypeStruct((rows, D_MODEL), F32),
        scratch_shapes=[pltpu.VMEM((tm, GM_HALF), F32), pltpu.VMEM((tm, GM_HALF), BF16)],
        compiler_params=_params(),
        name="gmlp_prompt",
    )(x, nw, w_in, vn, ws, bias, w_out)


def _gmlp_sample(x, nw, w_in, vn, ws_flat, bs_flat, w_out, nt):
    rows = x.shape[0]
    return pl.pallas_call(
        functools.partial(_gmlp_kernel, tm=rows, time_major=True, nt=nt),
        grid=(1,),
        in_specs=[
            _row_spec(rows, D_MODEL),
            _const_spec((1, D_MODEL)),
            _const_spec((D_MODEL, 2 * GM_HALF)),
            _const_spec((1, GM_HALF)),
            _SMEM_SPEC,
            _SMEM_SPEC,
            _const_spec((GM_HALF, D_MODEL)),
        ],
        out_specs=[_row_spec(rows, D_MODEL), _row_spec(rows, GM_HALF)],
        out_shape=[jax.ShapeDtypeStruct((rows, D_MODEL), F32), jax.ShapeDtypeStruct((rows, GM_HALF), F32)],
        scratch_shapes=[pltpu.VMEM((rows, GM_HALF), F32), pltpu.VMEM((rows, GM_HALF), BF16)],
        compiler_params=_params(),
        name="gmlp_sample",
    )(x, nw, w_in, vn, ws_flat, bs_flat, w_out)


def _ffn_kernel(*refs, tm, shift, prefix, keep, tiles_per_seq, has_state, with_kv):
    refs = list(refs)
    x_ref, p_ref, nf_ref, wg_ref, wu_ref, cw_ref, cb_ref, wd_ref, pn_ref, pg_ref, pp_ref = refs[:11]
    refs = refs[11:]
    if has_state:
        state_ref = refs.pop(0)
    if with_kv:
        kvn_ref, wkv_ref, rope_ref = refs[:3]
        o_ref, conv_ref, k_ref, v_ref, gbuf, act = refs[3:]
    else:
        fn_ref = refs[0]
        o_ref, conv_ref, gbuf, act = refs[1:]

    if has_state:
        gbuf[0:prefix, :] = state_ref[...]
    else:
        @pl.when(pl.program_id(0) % tiles_per_seq == 0)
        def _():
            gbuf[0:prefix, :] = jnp.zeros((prefix, D_FF), F32)

    x = x_ref[...]
    h = _rms(x, nf_ref[...]).astype(BF16)
    for n in range(FF_BLOCKS):
        sl = slice(n * MXU_COLS, (n + 1) * MXU_COLS)
        g = _dot(h, wg_ref[:, sl])
        gbuf[prefix:prefix + tm, sl] = g
        g1 = gbuf[prefix - shift:prefix - shift + tm, sl]
        g2 = gbuf[prefix - 2 * shift:prefix - 2 * shift + tm, sl]
        conv = cb_ref[:, sl] + g2 * cw_ref[0:1, sl]
        conv = conv + g1 * cw_ref[1:2, sl]
        conv = conv + g * cw_ref[2:3, sl]
        up = _dot(h, wu_ref[:, sl])
        act[:, sl] = (_gelu(conv) * up).astype(BF16)

    conv_ref[...] = gbuf[prefix + tm - keep:prefix + tm, :]
    if not has_state:
        gbuf[0:prefix, :] = gbuf[tm:tm + prefix, :]

    x = x + _dot(act[...], wd_ref[...])
    gate = jax.nn.sigmoid(_dot(_rms(x, pn_ref[...]).astype(BF16), pg_ref[...]))
    x = x + _dot(p_ref[...].astype(BF16), pp_ref[...]) * gate

    if with_kv:
        o_ref[...] = x
        kv = _dot(_rms(x, kvn_ref[...]).astype(BF16), wkv_ref[...])
        k_ref[...] = _rope(kv[:, 0:KV_DIM], rope_ref[...])
        v_ref[...] = kv[:, KV_DIM:2 * KV_DIM]
    else:
        o_ref[...] = _rms(x, fn_ref[...])


def _ffn(x, p, w, *, tm, time_steps=None, state=None, kv=None, final_norm=None):
    rows = x.shape[0]
    n_tiles = rows // tm
    has_state = state is not None
    with_kv = kv is not None
    time_major = time_steps is not None
    if time_major:
        nb = rows // time_steps
        shift, prefix, keep, tiles_per_seq = nb, 2 * nb, 2 * nb, 1
        conv_rows, conv_spec = keep, pl.BlockSpec((keep, D_FF), lambda i: (0, 0))
    else:
        shift, prefix, keep = 1, SUBLANES, SUBLANES
        tiles_per_seq = w["seq"] // tm
        conv_rows = (rows // w["seq"]) * keep
        conv_spec = pl.BlockSpec((keep, D_FF), lambda i: (i // tiles_per_seq, 0))

    args = [x, p, w["nf"], w["wg"], w["wu"], w["cw"], w["cb"], w["wd"], w["pn"], w["pg"], w["pp"]]
    in_specs = [
        _row_spec(tm, D_MODEL), _row_spec(tm, PLE_DIM), _const_spec((1, D_MODEL)),
        _const_spec((D_MODEL, D_FF)), _const_spec((D_MODEL, D_FF)), _const_spec((CONV_W, D_FF)),
        _const_spec((1, D_FF)), _const_spec((D_FF, D_MODEL)), _const_spec((1, D_MODEL)),
        _const_spec((D_MODEL, D_MODEL)), _const_spec((PLE_DIM, D_MODEL)),
    ]
    if has_state:
        args.append(state)
        in_specs.append(_const_spec((prefix, D_FF)))
    out_specs = [_row_spec(tm, D_MODEL), conv_spec]
    out_shape = [jax.ShapeDtypeStruct((rows, D_MODEL), F32), jax.ShapeDtypeStruct((conv_rows, D_FF), F32)]
    if with_kv:
        kvn, wkv, rope = kv
        args += [kvn, wkv, rope]
        rope_tiles = rope.shape[0] // tm
        in_specs += [_const_spec((1, D_MODEL)), _const_spec((D_MODEL, 2 * KV_DIM)),
                     pl.BlockSpec((tm, 3 * LANES), lambda i: (i % rope_tiles, 0))]
        out_specs += [_row_spec(tm, KV_DIM), _row_spec(tm, KV_DIM)]
        out_shape += [jax.ShapeDtypeStruct((rows, KV_DIM), F32)] * 2
    else:
        args.append(final_norm)
        in_specs.append(_const_spec((1, D_MODEL)))

    return pl.pallas_call(
        functools.partial(_ffn_kernel, tm=tm, shift=shift, prefix=prefix, keep=keep,
                          tiles_per_seq=tiles_per_seq, has_state=has_state, with_kv=with_kv),
        grid=(n_tiles,),
        in_specs=in_specs,
        out_specs=out_specs,
        out_shape=out_shape,
        scratch_shapes=[pltpu.VMEM((prefix + tm, D_FF), F32), pltpu.VMEM((tm, D_FF), BF16)],
        compiler_params=_params(),
        name=("ffn_kv" if with_kv else "ffn_final") + ("_sample" if time_major else "_prompt"),
    )(*args)


def _q_proj(x_ref, nm_ref, wq_ref, rope_ref, q_scr):
    h = _rms(x_ref[...], nm_ref[...]).astype(BF16)
    table = rope_ref[...]
    for s in range(SLABS):
        sl = slice(s * LANES, (s + 1) * LANES)
        qs = _rope(_dot(h, wq_ref[:, sl]), table) * (HEAD_DIM ** -0.5)
        q_scr[:, sl] = qs.astype(q_scr.dtype)


def _head_frames(kv):
    lane = lax.broadcasted_iota(jnp.int32, kv.shape, 1)
    lo = lane < HEAD_DIM
    sw = pltpu.roll(kv, HEAD_DIM, 1)
    zero = jnp.zeros_like(kv)
    return [jnp.where(lo, kv, zero), jnp.where(lo, zero, sw), jnp.where(lo, sw, zero), jnp.where(lo, zero, kv)]


def _softmax_with_sink(s, mask, sink):
    s = jnp.where(mask, s, NEG_INF)
    m = jnp.maximum(jnp.max(s, axis=-1, keepdims=True), sink)
    e = jnp.exp(s - m)
    den = jnp.sum(e, axis=-1, keepdims=True) + jnp.exp(sink - m)
    return e / den


def _attn_prompt_kernel(x_ref, nm_ref, wq_ref, rope_ref, kc_ref, kp_ref, vc_ref, vp_ref, sinks_ref, wo_ref,
                        o_ref, q_scr, kx, vx, a_scr, *, tm, tiles_per_seq):
    first_tile = pl.program_id(0) % tiles_per_seq == 0
    _q_proj(x_ref, nm_ref, wq_ref, rope_ref, q_scr)

    for f, fr in enumerate(_head_frames(jnp.concatenate([kp_ref[...], kc_ref[...]], axis=0))):
        kx[f] = fr.astype(BF16)
    for f, fr in enumerate(_head_frames(jnp.concatenate([vp_ref[...], vc_ref[...]], axis=0))):
        vx[f] = fr.astype(BF16)

    qi = lax.broadcasted_iota(jnp.int32, (WINDOW, 2 * WINDOW), 0)
    sj = lax.broadcasted_iota(jnp.int32, (WINDOW, 2 * WINDOW), 1)
    band = (sj > qi) & (sj <= qi + WINDOW)

    def q_block(qb, carry):
        r0 = pl.multiple_of(qb * WINDOW, WINDOW)
        lo_lim = jnp.where(jnp.logical_and(first_tile, qb == 0), WINDOW, 0)
        mask = band & (sj >= lo_lim)
        for j in range(N_KV_HEADS):
            qstack = jnp.concatenate(
                [q_scr[pl.ds(r0, WINDOW), (SLABS_PER_KV * j + s) * LANES:(SLABS_PER_KV * j + s + 1) * LANES]
                 for s in range(SLABS_PER_KV)], axis=0)
            probs = []
            for par in range(2):
                sc = _dot_nt(qstack, kx[2 * j + par, pl.ds(r0, 2 * WINDOW), :])
                per_slab = []
                for s in range(SLABS_PER_KV):
                    head = (SLABS_PER_KV * j + s) * 2 + par
                    p = _softmax_with_sink(sc[s * WINDOW:(s + 1) * WINDOW], mask, sinks_ref[head])
                    per_slab.append(p.astype(BF16))
                probs.append(per_slab)
            vcat = jnp.concatenate([vx[2 * j, pl.ds(r0, 2 * WINDOW), :], vx[2 * j + 1, pl.ds(r0, 2 * WINDOW), :]],
                                   axis=0)
            for s in range(SLABS_PER_KV):
                pcat = jnp.concatenate([probs[0][s], probs[1][s]], axis=1)
                col = (SLABS_PER_KV * j + s) * LANES
                a_scr[pl.ds(r0, WINDOW), col:col + LANES] = _dot(pcat, vcat).astype(BF16)
        return carry

    lax.fori_loop(0, tm // WINDOW, q_block, 0)
    o_ref[...] = x_ref[...] + _dot(a_scr[...], wo_ref[...])


def _attn_prompt(x, nm, wq, rope, k, v, sinks, wo, *, tm, seq):
    rows = x.shape[0]
    tiles_per_seq = seq // tm
    blocks_per_tile = tm // WINDOW
    cur = pl.BlockSpec((tm, KV_DIM), lambda i: (i, 0))
    prev = pl.BlockSpec((WINDOW, KV_DIM), lambda i: (jnp.maximum(i * blocks_per_tile - 1, 0), 0))
    return pl.pallas_call(
        functools.partial(_attn_prompt_kernel, tm=tm, tiles_per_seq=tiles_per_seq),
        grid=(rows // tm,),
        in_specs=[
            _row_spec(tm, D_MODEL), _const_spec((1, D_MODEL)), _const_spec((D_MODEL, D_MODEL)),
            pl.BlockSpec((tm, 3 * LANES), lambda i: (i % tiles_per_seq, 0)),
            cur, prev, cur, prev, _SMEM_SPEC, _const_spec((D_MODEL, D_MODEL)),
        ],
        out_specs=_row_spec(tm, D_MODEL),
        out_shape=jax.ShapeDtypeStruct((rows, D_MODEL), F32),
        scratch_shapes=[
            pltpu.VMEM((tm, D_MODEL), BF16),
            pltpu.VMEM((4, WINDOW + tm, LANES), BF16),
            pltpu.VMEM((4, WINDOW + tm, LANES), BF16),
            pltpu.VMEM((tm, D_MODEL), BF16),
        ],
        compiler_params=_params(),
        name="attn_prompt",
    )(x, nm, wq, rope, k, k, v, v, sinks, wo)


def _attn_sample_kernel(x_ref, nm_ref, wq_ref, rope_ref, ck_ref, cv_ref, kn_ref, vn_ref, sinks_ref, wo_ref,
                        o_ref, q_scr, kfr, vfr, a_scr, *, n_seq):
    n_new = kn_ref.shape[0] // n_seq
    _q_proj(x_ref, nm_ref, wq_ref, rope_ref, q_scr)
    kfr[...] = jnp.zeros(kfr.shape, F32)
    vfr[...] = jnp.zeros(vfr.shape, F32)

    frame = 2 * WINDOW
    qi = lax.broadcasted_iota(jnp.int32, (PAD_T, frame), 0)
    sj = lax.broadcasted_iota(jnp.int32, (PAD_T, frame), 1)
    mask = (sj > qi) & (sj <= qi + WINDOW) & (sj < WINDOW + n_new)

    def seq_body(b, carry):
        r0 = pl.multiple_of(b * PAD_T, PAD_T)
        for f, fr in enumerate(_head_frames(ck_ref[b])):
            kfr[f, 0:WINDOW, :] = fr
        for f, fr in enumerate(_head_frames(kn_ref[pl.ds(r0, PAD_T), :])):
            kfr[f, WINDOW:WINDOW + PAD_T, :] = fr
        for f, fr in enumerate(_head_frames(cv_ref[b])):
            vfr[f, 0:WINDOW, :] = fr
        for f, fr in enumerate(_head_frames(vn_ref[pl.ds(r0, PAD_T), :])):
            vfr[f, WINDOW:WINDOW + PAD_T, :] = fr
        for j in range(N_KV_HEADS):
            qstack = jnp.concatenate(
                [q_scr[pl.ds(r0, PAD_T), (SLABS_PER_KV * j + s) * LANES:(SLABS_PER_KV * j + s + 1) * LANES]
                 for s in range(SLABS_PER_KV)], axis=0).astype(BF16)
            probs = []
            for par in range(2):
                sc = _dot_nt(qstack, kfr[2 * j + par].astype(BF16))
                per_slab = []
                for s in range(SLABS_PER_KV):
                    head = (SLABS_PER_KV * j + s) * 2 + par
                    per_slab.append(_softmax_with_sink(sc[s * PAD_T:(s + 1) * PAD_T], mask, sinks_ref[head]))
                probs.append(jnp.concatenate(per_slab, axis=0))
            pcat = jnp.concatenate(probs, axis=1).astype(BF16)
            vcat = jnp.concatenate([vfr[2 * j], vfr[2 * j + 1]], axis=0).astype(BF16)
            o = _dot(pcat, vcat)
            for s in range(SLABS_PER_KV):
                col = (SLABS_PER_KV * j + s) * LANES
                a_scr[pl.ds(r0, PAD_T), col:col + LANES] = o[s * PAD_T:(s + 1) * PAD_T]
        return carry

    lax.fori_loop(0, n_seq, seq_body, 0)
    o_ref[...] = x_ref[...] + _dot(a_scr[...].astype(BF16), wo_ref[...])


def _attn_sample(x, nm, wq, rope, ck, cv, kn, vn, sinks, wo, *, n_seq):
    rows = x.shape[0]
    full = lambda shape: pl.BlockSpec(shape, lambda i: (0,) * len(shape))
    return pl.pallas_call(
        functools.partial(_attn_sample_kernel, n_seq=n_seq),
        grid=(1,),
        in_specs=[
            full((rows, D_MODEL)), _const_spec((1, D_MODEL)), _const_spec((D_MODEL, D_MODEL)),
            full((rows, 3 * LANES)), full(ck.shape), full(cv.shape), full(kn.shape), full(vn.shape),
            _SMEM_SPEC, _const_spec((D_MODEL, D_MODEL)),
        ],
        out_specs=full((rows, D_MODEL)),
        out_shape=jax.ShapeDtypeStruct((rows, D_MODEL), F32),
        scratch_shapes=[
            pltpu.VMEM((rows, D_MODEL), F32),
            pltpu.VMEM((4, 2 * WINDOW, LANES), F32),
            pltpu.VMEM((4, 2 * WINDOW, LANES), F32),
            pltpu.VMEM((rows, D_MODEL), F32),
        ],
        compiler_params=_params(),
        name="attn_sample",
    )(x, nm, wq, rope, ck, cv, kn, vn, sinks, wo)


def _rope_table(pos):
    rows = pos.shape[0]
    inv_freq = ROPE_THETA ** (-jnp.arange(ROT_HALF, dtype=F32) / ROT_HALF)
    ang = pos.astype(F32)[:, None] * inv_freq[None, :]
    cos, sin = jnp.cos(ang), jnp.sin(ang)
    zeros = lambda n: jnp.zeros((rows, n), F32)
    c = jnp.concatenate([cos, cos, jnp.ones((rows, HEAD_DIM - ROT_DIM), F32)], axis=1)
    s1 = jnp.concatenate([zeros(ROT_HALF), sin, zeros(HEAD_DIM - ROT_DIM)], axis=1)
    s2 = jnp.concatenate([-sin, zeros(HEAD_DIM - ROT_HALF)], axis=1)
    return jnp.concatenate([c, c, s1, s1, s2, s2], axis=1)


def kernel(x_prompt, x_sample, state_ffn_conv, cache_k_win, cache_v_win, p_prompt, p_sample, norm_mix, gm_w_in,
           gm_v_norm, gm_w_s, gm_b_s, gm_w_out, kv_norm, w_kv, w_q, attn_sinks, w_o, norm_ffn, ffn_w_gate,
           ffn_w_up, ffn_conv_w, ffn_conv_b, ffn_w_down, ple_norm, ple_w_gate, ple_w_proj, final_norm):
    batch, seq, _ = x_prompt.shape
    dec_batch, dec_seq, _ = x_sample.shape
    depth = norm_mix.shape[0]
    assert depth == 2 and gm_w_in.shape[0] == 1 and w_q.shape[0] == 1
    tm = 512
    assert seq % tm == 0 and tm % CHUNK == 0 and dec_seq <= PAD_T

    row = lambda a: a.reshape(1, -1)
    layers = []
    for i in range(depth):
        layers.append(dict(
            seq=seq, nf=row(norm_ffn[i]), wg=ffn_w_gate[i].astype(BF16), wu=ffn_w_up[i].astype(BF16),
            cw=ffn_conv_w[i], cb=row(ffn_conv_b[i]), wd=ffn_w_down[i].astype(BF16), pn=row(ple_norm[i]),
            pg=ple_w_gate[i].astype(BF16), pp=ple_w_proj[i].astype(BF16)))
    w_in = gm_w_in[0].astype(BF16)
    w_out = gm_w_out[0].astype(BF16)
    wq = w_q[0].astype(BF16)
    wo = w_o[0].astype(BF16)
    wkv = w_kv.astype(BF16)
    kvn = row(kv_norm)
    nm0, nm1 = row(norm_mix[0]), row(norm_mix[1])
    vn = row(gm_v_norm[0])
    fn = row(final_norm)
    sinks = attn_sinks[0]

    rope_p = _rope_table(jnp.arange(seq, dtype=jnp.int32))
    bias_p = jnp.repeat(gm_b_s[0].T, GM_GROUP_DIM, axis=1)
    xp = x_prompt.reshape(batch * seq, D_MODEL)
    pp0 = p_prompt[0].reshape(batch * seq, PLE_DIM)
    pp1 = p_prompt[1].reshape(batch * seq, PLE_DIM)
    xp = _gmlp_prompt(xp, nm0, w_in, vn, gm_w_s[0], bias_p, w_out, tm)
    xp, conv_p0, k_p, v_p = _ffn(xp, pp0, layers[0], tm=tm, kv=(kvn, wkv, rope_p))
    xp = _attn_prompt(xp, nm1, wq, rope_p, k_p, v_p, sinks, wo, tm=tm, seq=seq)
    y_p, conv_p1 = _ffn(xp, pp1, layers[1], tm=tm, final_norm=fn)

    y_prompt = y_p.reshape(batch, seq, D_MODEL)
    new_conv_prompt = jnp.stack(
        [c.reshape(batch, SUBLANES, D_FF)[:, SUBLANES - (CONV_W - 1):] for c in (conv_p0, conv_p1)], axis=0)
    new_k_prompt = k_p.reshape(batch, seq, N_KV_HEADS, HEAD_DIM)[:, seq - WINDOW:]
    new_v_prompt = v_p.reshape(batch, seq, N_KV_HEADS, HEAD_DIM)[:, seq - WINDOW:]

    n_rows = dec_batch * dec_seq
    to_tm = lambda a: jnp.swapaxes(a, 0, 1).reshape(a.shape[0] * a.shape[1], a.shape[2])
    from_tm = lambda a, t: jnp.swapaxes(a.reshape(t, dec_batch, a.shape[-1]), 0, 1)
    pos_s = PAST_LEN + jnp.arange(dec_seq, dtype=jnp.int32)
    rope_s = _rope_table(jnp.repeat(pos_s, dec_batch))
    xs = to_tm(x_sample)
    ws_flat = gm_w_s[0][:, :dec_seq, :dec_seq].reshape(-1)
    bs_flat = gm_b_s[0][:, :dec_seq].reshape(-1)
    xs, gm_v = _gmlp_sample(xs, nm0, w_in, vn, ws_flat, bs_flat, w_out, dec_seq)
    xs, conv_s0, k_s, v_s = _ffn(xs, to_tm(p_sample[0]), layers[0], tm=n_rows, time_steps=dec_seq,
                                 state=to_tm(state_ffn_conv[0]), kv=(kvn, wkv, rope_s))

    pad_bm = lambda a: jnp.pad(from_tm(a, dec_seq), ((0, 0), (0, PAD_T - dec_seq), (0, 0))).reshape(
        dec_batch * PAD_T, a.shape[-1])
    rope_a = _rope_table(jnp.tile(PAST_LEN + jnp.arange(PAD_T, dtype=jnp.int32), dec_batch))
    xa = _attn_sample(pad_bm(xs), nm1, wq, rope_a,
                      cache_k_win.reshape(dec_batch, WINDOW, KV_DIM), cache_v_win.reshape(dec_batch, WINDOW, KV_DIM),
                      pad_bm(k_s), pad_bm(v_s), sinks, wo, n_seq=dec_batch)
    xs = to_tm(xa.reshape(dec_batch, PAD_T, D_MODEL)[:, :dec_seq])
    y_s, conv_s1 = _ffn(xs, to_tm(p_sample[1]), layers[1], tm=n_rows, time_steps=dec_seq,
                        state=to_tm(state_ffn_conv[1]), final_norm=fn)

    y_sample = from_tm(y_s, dec_seq)
    new_gm_v_sample = from_tm(gm_v, dec_seq)[None]
    new_conv_sample = jnp.stack([from_tm(c, CONV_W - 1) for c in (conv_s0, conv_s1)], axis=0)
    new_k_sample = from_tm(k_s, dec_seq).reshape(dec_batch, dec_seq, N_KV_HEADS, HEAD_DIM)
    new_v_sample = from_tm(v_s, dec_seq).reshape(dec_batch, dec_seq, N_KV_HEADS, HEAD_DIM)

    return (y_prompt, y_sample, new_gm_v_sample, new_conv_prompt, new_conv_sample,
            new_k_prompt, new_v_prompt, new_k_sample, new_v_sample)
```

```python
import functools

import numpy as np
import jax
import jax.numpy as jnp
from jax import lax
from jax.experimental import pallas as pl
from jax.experimental.pallas import tpu as pltpu

D_MODEL = 1024
GM_HALF = 2 * D_MODEL
GM_GROUPS = 8
GM_GROUP_DIM = GM_HALF // GM_GROUPS
CHUNK = 128
HEAD_DIM = 64
N_HEADS = D_MODEL // HEAD_DIM
N_KV_HEADS = N_HEADS // 8
KV_DIM = N_KV_HEADS * HEAD_DIM
WINDOW = 128
ROT_DIM = HEAD_DIM // 4
ROT_HALF = ROT_DIM // 2
ROPE_THETA = 500000.0
D_FF = 2816
CONV_W = 3
PLE_DIM = 256
EPS = 1e-6
NEG_INF = -1e30
PAST_LEN = 16384

LANES = 128
SUBLANES = 8
MXU_COLS = 256
VMEM_LIMIT_BYTES = 56 * 1024 * 1024
SLABS = D_MODEL // LANES
SLABS_PER_KV = SLABS // N_KV_HEADS
FF_BLOCKS = D_FF // MXU_COLS
PAD_T = SUBLANES
SEQ_UNROLL = 4

BF16 = jnp.bfloat16
F32 = jnp.float32


def _rms(x, w):
    ms = jnp.sum(x * x, axis=-1, keepdims=True) * (1.0 / x.shape[-1])
    return x * lax.rsqrt(ms + EPS) * w


def _dot(a, b):
    return jnp.dot(a, b, preferred_element_type=F32)


def _dot_nt(a, b):
    return lax.dot_general(a, b, (((1,), (1,)), ((), ())), preferred_element_type=F32)


def _gelu(x):
    return jax.nn.gelu(x, approximate=True)


def _rope(x, table):
    c = table[:, 0:LANES]
    s = table[:, LANES:2 * LANES]
    lane = lax.broadcasted_iota(jnp.int32, x.shape, 1)
    partner = jnp.where((lane & ROT_HALF) == 0, pltpu.roll(x, LANES - ROT_HALF, 1), pltpu.roll(x, ROT_HALF, 1))
    return x * c + partner * s


def _const_spec(shape):
    zeros = (0,) * len(shape)
    return pl.BlockSpec(shape, lambda i: zeros, pipeline_mode=pl.Buffered(1))


def _layer_spec(shape, layer):
    idx = (layer,) + (0,) * len(shape)
    return pl.BlockSpec((None,) + tuple(shape), lambda i: idx, pipeline_mode=pl.Buffered(1))


def _row_spec(tm, width, offset=0):
    return pl.BlockSpec((tm, width), lambda i: (i + offset, 0))


_SMEM_SPEC = pl.BlockSpec(memory_space=pltpu.SMEM)


def _params():
    return pltpu.CompilerParams(dimension_semantics=("arbitrary",), vmem_limit_bytes=VMEM_LIMIT_BYTES)


def _gmlp_kernel(*refs, tm, time_major, nt=None):
    if time_major:
        x_ref, nw_ref, win_ref, vn_ref, ws_ref, bs_ref, wout_ref, o_ref, v_ref, v_scr, out_scr = refs
    else:
        x_ref, nw_ref, win_ref, vn_ref, ws_ref, bias_ref, wout_ref, o_ref, v_scr, out_scr = refs

    x = x_ref[...]
    h = _rms(x, nw_ref[...]).astype(BF16)

    ssq = jnp.zeros((tm, 1), F32)
    for g in range(GM_GROUPS):
        sl = slice(g * GM_GROUP_DIM, (g + 1) * GM_GROUP_DIM)
        vg = _gelu(_dot(h, win_ref[:, GM_HALF + g * GM_GROUP_DIM:GM_HALF + (g + 1) * GM_GROUP_DIM]))
        v_scr[:, sl] = vg
        ssq = ssq + jnp.sum(vg * vg, axis=-1, keepdims=True)
    rinv = lax.rsqrt(ssq * (1.0 / GM_HALF) + EPS)

    if not time_major:
        row = lax.broadcasted_iota(jnp.int32, (CHUNK, CHUNK), 0)
        col = lax.broadcasted_iota(jnp.int32, (CHUNK, CHUNK), 1)
        causal = col <= row

    for g in range(GM_GROUPS):
        sl = slice(g * GM_GROUP_DIM, (g + 1) * GM_GROUP_DIM)
        vg = v_scr[:, sl] * rinv * vn_ref[:, sl]
        ug = _gelu(_dot(h, win_ref[:, sl]))
        if time_major:
            v_ref[:, sl] = vg
            nb = tm // nt
            for t in range(nt):
                m = ws_ref[(g * nt + t) * nt] * vg[0:nb]
                for s in range(1, t + 1):
                    m = m + ws_ref[(g * nt + t) * nt + s] * vg[s * nb:(s + 1) * nb]
                m = m + bs_ref[g * nt + t]
                out_scr[t * nb:(t + 1) * nb, sl] = (ug[t * nb:(t + 1) * nb] * m).astype(BF16)
        else:
            vb = vg.astype(BF16)
            wm = jnp.where(causal, ws_ref[g], 0.0).astype(BF16)
            for c in range(tm // CHUNK):
                rs = slice(c * CHUNK, (c + 1) * CHUNK)
                m = _dot(wm, vb[rs]) + bias_ref[:, sl]
                out_scr[rs, sl] = (ug[rs] * m).astype(BF16)

    o_ref[...] = x + _dot(out_scr[...], wout_ref[...])


def _gmlp(x, nw, w_in, vn, w_out, *, tm, ws=None, bias=None, ws_flat=None, bs_flat=None, nt=None):
    rows = x.shape[0]
    time_major = ws is None
    in_specs = [_row_spec(tm, D_MODEL), _layer_spec((1, D_MODEL), 0), _layer_spec((D_MODEL, 2 * GM_HALF), 0),
                _layer_spec((1, GM_HALF), 0)]
    out_specs = [_row_spec(tm, D_MODEL)]
    out_shape = [jax.ShapeDtypeStruct((rows, D_MODEL), F32)]
    if time_major:
        args = [x, nw, w_in, vn, ws_flat, bs_flat, w_out]
        in_specs += [_SMEM_SPEC, _SMEM_SPEC]
        out_specs.append(_row_spec(tm, GM_HALF))
        out_shape.append(jax.ShapeDtypeStruct((rows, GM_HALF), F32))
    else:
        args = [x, nw, w_in, vn, ws, bias, w_out]
        in_specs += [_layer_spec((GM_GROUPS, CHUNK, CHUNK), 0), _const_spec((CHUNK, GM_HALF))]
    in_specs.append(_layer_spec((GM_HALF, D_MODEL), 0))
    return pl.pallas_call(
        functools.partial(_gmlp_kernel, tm=tm, time_major=time_major, nt=nt),
        grid=(rows // tm,),
        in_specs=in_specs,
        out_specs=out_specs,
        out_shape=out_shape,
        scratch_shapes=[pltpu.VMEM((tm, GM_HALF), F32), pltpu.VMEM((tm, GM_HALF), BF16)],
        compiler_params=_params(),
        name="gmlp_sample" if time_major else "gmlp_prompt",
    )(*args)


def _ffn_kernel(*refs, tm, shift, prefix, tiles_per_seq, has_state, with_kv):
    refs = list(refs)
    x_ref, p_ref, nf_ref, wg_ref, wu_ref, cw_ref, cb_ref, wd_ref, pn_ref, pg_ref, pp_ref = refs[:11]
    refs = refs[11:]
    if has_state:
        state_ref = refs.pop(0)
    if with_kv:
        kvn_ref, wkv_ref, rope_ref = refs[:3]
        o_ref, conv_ref, k_ref, v_ref, gbuf, act = refs[3:]
    else:
        fn_ref = refs[0]
        o_ref, conv_ref, gbuf, act = refs[1:]

    if has_state:
        gbuf[0:prefix, :] = state_ref[...]
    else:
        @pl.when(pl.program_id(0) % tiles_per_seq == 0)
        def _():
            gbuf[0:prefix, :] = jnp.zeros((prefix, D_FF), F32)

    x = x_ref[...]
    h = _rms(x, nf_ref[...]).astype(BF16)
    for n in range(FF_BLOCKS):
        sl = slice(n * MXU_COLS, (n + 1) * MXU_COLS)
        g = _dot(h, wg_ref[:, sl])
        gbuf[prefix:prefix + tm, sl] = g
        g1 = gbuf[prefix - shift:prefix - shift + tm, sl]
        g2 = gbuf[prefix - 2 * shift:prefix - 2 * shift + tm, sl]
        conv = cb_ref[:, sl] + g2 * cw_ref[0:1, sl]
        conv = conv + g1 * cw_ref[1:2, sl]
        conv = conv + g * cw_ref[2:3, sl]
        up = _dot(h, wu_ref[:, sl])
        act[:, sl] = (_gelu(conv) * up).astype(BF16)

    conv_ref[...] = gbuf[tm:tm + prefix, :]
    if not has_state:
        gbuf[0:prefix, :] = gbuf[tm:tm + prefix, :]

    x = x + _dot(act[...], wd_ref[...])
    gate = jax.nn.sigmoid(_dot(_rms(x, pn_ref[...]).astype(BF16), pg_ref[...]))
    x = x + _dot(p_ref[...].astype(BF16), pp_ref[...]) * gate

    if with_kv:
        o_ref[...] = x
        kv = _dot(_rms(x, kvn_ref[...]).astype(BF16), wkv_ref[...])
        k_ref[...] = _rope(kv[:, 0:KV_DIM], rope_ref[...])
        v_ref[...] = kv[:, KV_DIM:2 * KV_DIM]
    else:
        o_ref[...] = _rms(x, fn_ref[...])


def _ffn(x, p, w, layer, *, tm, seq=None, p_offset=0, time_steps=None, state=None, kv=None, final_norm=None):
    rows = x.shape[0]
    has_state = state is not None
    with_kv = kv is not None
    time_major = time_steps is not None
    if time_major:
        shift = rows // time_steps
        prefix, tiles_per_seq = (CONV_W - 1) * shift, 1
        conv_rows, conv_spec = prefix, pl.BlockSpec((prefix, D_FF), lambda i: (0, 0))
    else:
        shift, prefix = 1, SUBLANES
        tiles_per_seq = seq // tm
        conv_rows = (rows // seq) * prefix
        conv_spec = pl.BlockSpec((prefix, D_FF), lambda i: (i // tiles_per_seq, 0))

    args = [x, p, w["nf"], w["wg"], w["wu"], w["cw"], w["cb"], w["wd"], w["pn"], w["pg"], w["pp"]]
    in_specs = [
        _row_spec(tm, D_MODEL), _row_spec(tm, PLE_DIM, p_offset), _layer_spec((1, D_MODEL), layer),
        _layer_spec((D_MODEL, D_FF), layer), _layer_spec((D_MODEL, D_FF), layer),
        _layer_spec((CONV_W, D_FF), layer), _layer_spec((1, D_FF), layer), _layer_spec((D_FF, D_MODEL), layer),
        _layer_spec((1, D_MODEL), layer), _layer_spec((D_MODEL, D_MODEL), layer),
        _layer_spec((PLE_DIM, D_MODEL), layer),
    ]
    scratch = [pltpu.VMEM((prefix + tm, D_FF), F32), pltpu.VMEM((tm, D_FF), BF16)]
    if has_state:
        args.append(state)
        in_specs.append(_const_spec((prefix, D_FF)))
    out_specs = [_row_spec(tm, D_MODEL), conv_spec]
    out_shape = [jax.ShapeDtypeStruct((rows, D_MODEL), F32), jax.ShapeDtypeStruct((conv_rows, D_FF), F32)]
    if with_kv:
        kvn, wkv, rope = kv
        args += [kvn, wkv, rope]
        rope_tiles = rope.shape[0] // tm
        in_specs += [_const_spec((1, D_MODEL)), _const_spec((D_MODEL, 2 * KV_DIM)),
                     pl.BlockSpec((tm, 2 * LANES), lambda i: (i % rope_tiles, 0))]
        out_specs += [_row_spec(tm, KV_DIM), _row_spec(tm, KV_DIM)]
        out_shape += [jax.ShapeDtypeStruct((rows, KV_DIM), F32)] * 2
    else:
        args.append(final_norm)
        in_specs.append(_const_spec((1, D_MODEL)))

    return pl.pallas_call(
        functools.partial(_ffn_kernel, tm=tm, shift=shift, prefix=prefix, tiles_per_seq=tiles_per_seq,
                          has_state=has_state, with_kv=with_kv),
        grid=(rows // tm,),
        in_specs=in_specs,
        out_specs=out_specs,
        out_shape=out_shape,
        scratch_shapes=scratch,
        compiler_params=_params(),
        name=("ffn_kv" if with_kv else "ffn_final") + ("_sample" if time_major else "_prompt"),
    )(*args)


def _q_proj(x_ref, nm_ref, wq_ref, rope_ref, q_scr):
    h = _rms(x_ref[...], nm_ref[...]).astype(BF16)
    table = rope_ref[...]
    for c in range(D_MODEL // MXU_COLS):
        qc = _dot(h, wq_ref[:, c * MXU_COLS:(c + 1) * MXU_COLS])
        for s in range(MXU_COLS // LANES):
            col = c * MXU_COLS + s * LANES
            qs = _rope(qc[:, s * LANES:(s + 1) * LANES], table) * (HEAD_DIM ** -0.5)
            q_scr[:, col:col + LANES] = qs.astype(q_scr.dtype)


def _head_frames(kv):
    lane = lax.broadcasted_iota(jnp.int32, kv.shape, 1)
    lo = lane < HEAD_DIM
    sw = pltpu.roll(kv, HEAD_DIM, 1)
    zero = jnp.zeros_like(kv)
    return [jnp.where(lo, kv, zero), jnp.where(lo, zero, sw), jnp.where(lo, sw, zero), jnp.where(lo, zero, kv)]


def _softmax_parts(s, mask, sink):
    s = jnp.where(mask, s, NEG_INF)
    m = jnp.maximum(jnp.max(s, axis=-1, keepdims=True), sink)
    e = jnp.exp(s - m)
    den = jnp.sum(e, axis=-1, keepdims=True) + jnp.exp(sink - m)
    return e, 1.0 / den


def _attend(qstack, k_lo, k_hi, v_lo, v_hi, mask, sinks_ref, first_slab, rows):
    weights, scales = [], []
    for par, k_fr in enumerate((k_lo, k_hi)):
        sc = _dot_nt(qstack, k_fr)
        w_par, r_par = [], []
        for s in range(SLABS_PER_KV):
            head = (first_slab + s) * 2 + par
            e, r = _softmax_parts(sc[s * rows:(s + 1) * rows], mask, sinks_ref[head])
            w_par.append(e.astype(BF16))
            r_par.append(r)
        weights.append(w_par)
        scales.append(r_par)
    vcat = jnp.concatenate([v_lo, v_hi], axis=0)
    lane = lax.broadcasted_iota(jnp.int32, (rows, LANES), 1)
    outs = []
    for s in range(SLABS_PER_KV):
        o = _dot(jnp.concatenate([weights[0][s], weights[1][s]], axis=1), vcat)
        outs.append(o * jnp.where(lane < HEAD_DIM, scales[0][s], scales[1][s]))
    return outs


def _attn_prompt_kernel(x_ref, nm_ref, wq_ref, rope_ref, kc_ref, kp_ref, vc_ref, vp_ref, sinks_ref, wo_ref,
                        o_ref, q_scr, kx, vx, a_scr, *, tm, tiles_per_seq):
    first_tile = pl.program_id(0) % tiles_per_seq == 0
    _q_proj(x_ref, nm_ref, wq_ref, rope_ref, q_scr)

    for f, fr in enumerate(_head_frames(jnp.concatenate([kp_ref[...], kc_ref[...]], axis=0))):
        kx[f] = fr.astype(BF16)
    for f, fr in enumerate(_head_frames(jnp.concatenate([vp_ref[...], vc_ref[...]], axis=0))):
        vx[f] = fr.astype(BF16)

    qi = lax.broadcasted_iota(jnp.int32, (WINDOW, 2 * WINDOW), 0)
    sj = lax.broadcasted_iota(jnp.int32, (WINDOW, 2 * WINDOW), 1)
    band = (sj > qi) & (sj <= qi + WINDOW)

    def q_block(qb, carry):
        r0 = pl.multiple_of(qb * WINDOW, WINDOW)
        lo_lim = jnp.where(jnp.logical_and(first_tile, qb == 0), WINDOW, 0)
        mask = band & (sj >= lo_lim)
        for j in range(N_KV_HEADS):
            first_slab = SLABS_PER_KV * j
            qstack = jnp.concatenate(
                [q_scr[pl.ds(r0, WINDOW), (first_slab + s) * LANES:(first_slab + s + 1) * LANES]
                 for s in range(SLABS_PER_KV)], axis=0)
            keys = pl.ds(r0, 2 * WINDOW)
            outs = _attend(qstack, kx[2 * j, keys, :], kx[2 * j + 1, keys, :], vx[2 * j, keys, :],
                           vx[2 * j + 1, keys, :], mask, sinks_ref, first_slab, WINDOW)
            for s, o in enumerate(outs):
                col = (first_slab + s) * LANES
                a_scr[pl.ds(r0, WINDOW), col:col + LANES] = o.astype(BF16)
        return carry

    lax.fori_loop(0, tm // WINDOW, q_block, 0)
    o_ref[...] = x_ref[...] + _dot(a_scr[...], wo_ref[...])


def _attn_prompt(x, nm, wq, rope, k, v, sinks, wo, *, tm, seq):
    rows = x.shape[0]
    tiles_per_seq = seq // tm
    blocks_per_tile = tm // WINDOW
    cur = pl.BlockSpec((tm, KV_DIM), lambda i: (i, 0))
    prev = pl.BlockSpec((WINDOW, KV_DIM), lambda i: (jnp.maximum(i * blocks_per_tile - 1, 0), 0))
    return pl.pallas_call(
        functools.partial(_attn_prompt_kernel, tm=tm, tiles_per_seq=tiles_per_seq),
        grid=(rows // tm,),
        in_specs=[
            _row_spec(tm, D_MODEL), _layer_spec((1, D_MODEL), 1), _layer_spec((D_MODEL, D_MODEL), 0),
            pl.BlockSpec((tm, 2 * LANES), lambda i: (i % tiles_per_seq, 0)),
            cur, prev, cur, prev, _SMEM_SPEC, _layer_spec((D_MODEL, D_MODEL), 0),
        ],
        out_specs=_row_spec(tm, D_MODEL),
        out_shape=jax.ShapeDtypeStruct((rows, D_MODEL), F32),
        scratch_shapes=[
            pltpu.VMEM((tm, D_MODEL), BF16),
            pltpu.VMEM((4, WINDOW + tm, LANES), BF16),
            pltpu.VMEM((4, WINDOW + tm, LANES), BF16),
            pltpu.VMEM((tm, D_MODEL), BF16),
        ],
        compiler_params=_params(),
        name="attn_prompt",
    )(x, nm, wq, rope, k, k, v, v, sinks, wo)


def _attn_sample_kernel(x_ref, nm_ref, wq_ref, rope_ref, ck_ref, cv_ref, kn_ref, vn_ref, sinks_ref, wo_ref,
                        o_ref, q_scr, kfr, vfr, a_scr, *, n_seq):
    n_new = kn_ref.shape[0] // n_seq
    _q_proj(x_ref, nm_ref, wq_ref, rope_ref, q_scr)
    kfr[...] = jnp.zeros(kfr.shape, F32)
    vfr[...] = jnp.zeros(vfr.shape, F32)

    frame = 2 * WINDOW
    qi = lax.broadcasted_iota(jnp.int32, (PAD_T, frame), 0)
    sj = lax.broadcasted_iota(jnp.int32, (PAD_T, frame), 1)
    mask = (sj > qi) & (sj <= qi + WINDOW) & (sj < WINDOW + n_new)

    def one_seq(b, u):
        r0 = pl.multiple_of(b * PAD_T, PAD_T)
        for f, fr in enumerate(_head_frames(ck_ref[b])):
            kfr[u, f, 0:WINDOW, :] = fr
        for f, fr in enumerate(_head_frames(kn_ref[pl.ds(r0, PAD_T), :])):
            kfr[u, f, WINDOW:WINDOW + PAD_T, :] = fr
        for f, fr in enumerate(_head_frames(cv_ref[b])):
            vfr[u, f, 0:WINDOW, :] = fr
        for f, fr in enumerate(_head_frames(vn_ref[pl.ds(r0, PAD_T), :])):
            vfr[u, f, WINDOW:WINDOW + PAD_T, :] = fr
        for j in range(N_KV_HEADS):
            first_slab = SLABS_PER_KV * j
            qstack = jnp.concatenate(
                [q_scr[pl.ds(r0, PAD_T), (first_slab + s) * LANES:(first_slab + s + 1) * LANES]
                 for s in range(SLABS_PER_KV)], axis=0).astype(BF16)
            outs = _attend(qstack, kfr[u, 2 * j].astype(BF16), kfr[u, 2 * j + 1].astype(BF16),
                           vfr[u, 2 * j].astype(BF16), vfr[u, 2 * j + 1].astype(BF16), mask, sinks_ref,
                           first_slab, PAD_T)
            for s, o in enumerate(outs):
                col = (first_slab + s) * LANES
                a_scr[pl.ds(r0, PAD_T), col:col + LANES] = o

    def seq_group(g, carry):
        for u in range(SEQ_UNROLL):
            one_seq(g * SEQ_UNROLL + u, u)
        return carry

    lax.fori_loop(0, n_seq // SEQ_UNROLL, seq_group, 0)
    o_ref[...] = x_ref[...] + _dot(a_scr[...].astype(BF16), wo_ref[...])


def _attn_sample(x, nm, wq, rope, ck, cv, kn, vn, sinks, wo, *, n_seq):
    rows = x.shape[0]
    full = lambda shape: pl.BlockSpec(shape, lambda i: (0,) * len(shape))
    return pl.pallas_call(
        functools.partial(_attn_sample_kernel, n_seq=n_seq),
        grid=(1,),
        in_specs=[
            full((rows, D_MODEL)), _layer_spec((1, D_MODEL), 1), _layer_spec((D_MODEL, D_MODEL), 0),
            full((rows, 2 * LANES)), full(ck.shape), full(cv.shape), full(kn.shape), full(vn.shape),
            _SMEM_SPEC, _layer_spec((D_MODEL, D_MODEL), 0),
        ],
        out_specs=full((rows, D_MODEL)),
        out_shape=jax.ShapeDtypeStruct((rows, D_MODEL), F32),
        scratch_shapes=[
            pltpu.VMEM((rows, D_MODEL), F32),
            pltpu.VMEM((SEQ_UNROLL, 4, 2 * WINDOW, LANES), F32),
            pltpu.VMEM((SEQ_UNROLL, 4, 2 * WINDOW, LANES), F32),
            pltpu.VMEM((rows, D_MODEL), F32),
        ],
        compiler_params=_params(),
        name="attn_sample",
    )(x, nm, wq, rope, ck, cv, kn, vn, sinks, wo)


def _rope_table(pos):
    dim = np.arange(LANES) % HEAD_DIM
    inv_freq = ROPE_THETA ** (-jnp.arange(ROT_HALF, dtype=F32) / ROT_HALF)
    ang = pos.astype(F32)[:, None] * inv_freq[dim % ROT_HALF][None, :]
    in_rot = jnp.asarray(dim < ROT_DIM)[None, :]
    first = jnp.asarray(dim < ROT_HALF)[None, :]
    sin = jnp.sin(ang)
    c = jnp.where(in_rot, jnp.cos(ang), 1.0)
    s = jnp.where(first, -sin, jnp.where(in_rot, sin, 0.0))
    return jnp.concatenate([c, s], axis=1)


def kernel(x_prompt, x_sample, state_ffn_conv, cache_k_win, cache_v_win, p_prompt, p_sample, norm_mix, gm_w_in,
           gm_v_norm, gm_w_s, gm_b_s, gm_w_out, kv_norm, w_kv, w_q, attn_sinks, w_o, norm_ffn, ffn_w_gate,
           ffn_w_up, ffn_conv_w, ffn_conv_b, ffn_w_down, ple_norm, ple_w_gate, ple_w_proj, final_norm):
    batch, seq, _ = x_prompt.shape
    dec_batch, dec_seq, _ = x_sample.shape
    depth = norm_mix.shape[0]
    assert depth == 2 and gm_w_in.shape[0] == 1 and w_q.shape[0] == 1
    tm, tm_ffn = 1024, 512
    assert seq % tm == 0 and seq % tm_ffn == 0 and dec_seq <= PAD_T and dec_batch % SEQ_UNROLL == 0

    rows3 = lambda a: a.reshape(a.shape[0], 1, a.shape[1])
    ffn_w = dict(
        nf=rows3(norm_ffn), wg=ffn_w_gate.astype(BF16), wu=ffn_w_up.astype(BF16), cw=ffn_conv_w,
        cb=rows3(ffn_conv_b), wd=ffn_w_down.astype(BF16), pn=rows3(ple_norm), pg=ple_w_gate.astype(BF16),
        pp=ple_w_proj.astype(BF16))
    w_in = gm_w_in.astype(BF16)
    w_out = gm_w_out.astype(BF16)
    wq = w_q.astype(BF16)
    wo = w_o.astype(BF16)
    wkv = w_kv.astype(BF16)
    kvn = kv_norm.reshape(1, D_MODEL)
    nm = rows3(norm_mix)
    vn = rows3(gm_v_norm)
    fn = final_norm.reshape(1, D_MODEL)
    sinks = attn_sinks[0]

    n_rows_p = batch * seq
    rope_p = _rope_table(jnp.arange(seq, dtype=jnp.int32))
    bias_p = jnp.repeat(gm_b_s[0].T, GM_GROUP_DIM, axis=1)
    xp = x_prompt.reshape(n_rows_p, D_MODEL)
    p_p = p_prompt.reshape(depth * n_rows_p, PLE_DIM)
    (xp,) = _gmlp(xp, nm, w_in, vn, w_out, tm=tm, ws=gm_w_s, bias=bias_p)
    xp, conv_p0, k_p, v_p = _ffn(xp, p_p, ffn_w, 0, tm=tm_ffn, seq=seq, kv=(kvn, wkv, rope_p))
    xp = _attn_prompt(xp, nm, wq, rope_p, k_p, v_p, sinks, wo, tm=tm, seq=seq)
    y_p, conv_p1 = _ffn(xp, p_p, ffn_w, 1, tm=tm_ffn, seq=seq, p_offset=n_rows_p // tm_ffn, final_norm=fn)

    y_prompt = y_p.reshape(batch, seq, D_MODEL)
    new_conv_prompt = jnp.stack(
        [c.reshape(batch, SUBLANES, D_FF)[:, SUBLANES - (CONV_W - 1):] for c in (conv_p0, conv_p1)], axis=0)
    tail = lambda a: a.reshape(batch, seq, KV_DIM)[:, seq - WINDOW:].reshape(batch, WINDOW, N_KV_HEADS, HEAD_DIM)
    new_k_prompt = tail(k_p)
    new_v_prompt = tail(v_p)

    n_rows = dec_batch * dec_seq
    to_tm = lambda a: jnp.swapaxes(a, 0, 1).reshape(a.shape[0] * a.shape[1], a.shape[2])
    from_tm = lambda a, t: jnp.swapaxes(a.reshape(t, dec_batch, a.shape[-1]), 0, 1)
    pos_s = PAST_LEN + jnp.arange(dec_seq, dtype=jnp.int32)
    rope_s = _rope_table(jnp.repeat(pos_s, dec_batch))
    xs = to_tm(x_sample)
    ws_flat = gm_w_s[0][:, :dec_seq, :dec_seq].reshape(-1)
    bs_flat = gm_b_s[0][:, :dec_seq].reshape(-1)
    xs, gm_v = _gmlp(xs, nm, w_in, vn, w_out, tm=n_rows, ws_flat=ws_flat, bs_flat=bs_flat, nt=dec_seq)
    xs, conv_s0, k_s, v_s = _ffn(xs, to_tm(p_sample[0]), ffn_w, 0, tm=n_rows, time_steps=dec_seq,
                                 state=to_tm(state_ffn_conv[0]), kv=(kvn, wkv, rope_s))

    pad_bm = lambda a: jnp.pad(from_tm(a, dec_seq), ((0, 0), (0, PAD_T - dec_seq), (0, 0))).reshape(
        dec_batch * PAD_T, a.shape[-1])
    rope_a = _rope_table(jnp.tile(PAST_LEN + jnp.arange(PAD_T, dtype=jnp.int32), dec_batch))
    xa = _attn_sample(pad_bm(xs), nm, wq, rope_a,
                      cache_k_win.reshape(dec_batch, WINDOW, KV_DIM), cache_v_win.reshape(dec_batch, WINDOW, KV_DIM),
                      pad_bm(k_s), pad_bm(v_s), sinks, wo, n_seq=dec_batch)
    xs = to_tm(xa.reshape(dec_batch, PAD_T, D_MODEL)[:, :dec_seq])
    y_s, conv_s1 = _ffn(xs, to_tm(p_sample[1]), ffn_w, 1, tm=n_rows, time_steps=dec_seq,
                        state=to_tm(state_ffn_conv[1]), final_norm=fn)

    y_sample = from_tm(y_s, dec_seq)
    new_gm_v_sample = from_tm(gm_v, dec_seq)[None]
    new_conv_sample = jnp.stack([from_tm(c, CONV_W - 1) for c in (conv_s0, conv_s1)], axis=0)
    new_k_sample = from_tm(k_s, dec_seq).reshape(dec_batch, dec_seq, N_KV_HEADS, HEAD_DIM)
    new_v_sample = from_tm(v_s, dec_seq).reshape(dec_batch, dec_seq, N_KV_HEADS, HEAD_DIM)

    return (y_prompt, y_sample, new_gm_v_sample, new_conv_prompt, new_conv_sample,
            new_k_prompt, new_v_prompt, new_k_sample, new_v_sample)
```

```python
import functools

import numpy as np
import jax
import jax.numpy as jnp
from jax import lax
from jax.experimental import pallas as pl
from jax.experimental.pallas import tpu as pltpu

D_MODEL = 1024
GM_HALF = 2 * D_MODEL
GM_GROUPS = 8
GM_GROUP_DIM = GM_HALF // GM_GROUPS
CHUNK = 128
HEAD_DIM = 64
N_HEADS = D_MODEL // HEAD_DIM
N_KV_HEADS = N_HEADS // 8
KV_DIM = N_KV_HEADS * HEAD_DIM
WINDOW = 128
ROT_DIM = HEAD_DIM // 4
ROT_HALF = ROT_DIM // 2
ROPE_THETA = 500000.0
D_FF = 2816
CONV_W = 3
PLE_DIM = 256
EPS = 1e-6
NEG_INF = -1e30
PAST_LEN = 16384

LANES = 128
SUBLANES = 8
MXU_COLS = 256
VMEM_LIMIT_BYTES = 56 * 1024 * 1024
SLABS = D_MODEL // LANES
SLABS_PER_KV = SLABS // N_KV_HEADS
FF_BLOCKS = D_FF // MXU_COLS
PAD_T = SUBLANES
SEQ_GROUP = 8
Q_BLOCK_UNROLL = 4

BF16 = jnp.bfloat16
F32 = jnp.float32


def _rms(x, w):
    ms = jnp.sum(x * x, axis=-1, keepdims=True) * (1.0 / x.shape[-1])
    return x * lax.rsqrt(ms + EPS) * w


def _dot(a, b):
    return jnp.dot(a, b, preferred_element_type=F32)


def _dot_nt(a, b):
    return lax.dot_general(a, b, (((1,), (1,)), ((), ())), preferred_element_type=F32)


def _gelu(x):
    return jax.nn.gelu(x, approximate=True)


def _rope(x, table):
    c = table[:, 0:LANES]
    s = table[:, LANES:2 * LANES]
    lane = lax.broadcasted_iota(jnp.int32, x.shape, 1)
    partner = jnp.where((lane & ROT_HALF) == 0, pltpu.roll(x, LANES - ROT_HALF, 1), pltpu.roll(x, ROT_HALF, 1))
    return x * c + partner * s


def _const_spec(shape):
    zeros = (0,) * len(shape)
    return pl.BlockSpec(shape, lambda i: zeros, pipeline_mode=pl.Buffered(1))


def _layer_spec(shape, layer):
    idx = (layer,) + (0,) * len(shape)
    return pl.BlockSpec((None,) + tuple(shape), lambda i: idx, pipeline_mode=pl.Buffered(1))


def _row_spec(tm, width, offset=0):
    return pl.BlockSpec((tm, width), lambda i: (i + offset, 0))


_SMEM_SPEC = pl.BlockSpec(memory_space=pltpu.SMEM)


def _params():
    return pltpu.CompilerParams(dimension_semantics=("arbitrary",), vmem_limit_bytes=VMEM_LIMIT_BYTES)


def _gmlp_kernel(*refs, tm, time_major, nt=None):
    if time_major:
        x_ref, nw_ref, win_ref, vn_ref, ws_ref, bs_ref, wout_ref, o_ref, v_ref, v_scr, out_scr = refs
    else:
        x_ref, nw_ref, win_ref, vn_ref, ws_ref, bias_ref, wout_ref, o_ref, v_scr, out_scr = refs

    x = x_ref[...]
    h = _rms(x, nw_ref[...]).astype(BF16)

    ssq = jnp.zeros((tm, 1), F32)
    for g in range(GM_GROUPS):
        sl = slice(g * GM_GROUP_DIM, (g + 1) * GM_GROUP_DIM)
        vg = _gelu(_dot(h, win_ref[:, GM_HALF + g * GM_GROUP_DIM:GM_HALF + (g + 1) * GM_GROUP_DIM]))
        v_scr[:, sl] = vg
        ssq = ssq + jnp.sum(vg * vg, axis=-1, keepdims=True)
    rinv = lax.rsqrt(ssq * (1.0 / GM_HALF) + EPS)

    if not time_major:
        row = lax.broadcasted_iota(jnp.int32, (CHUNK, CHUNK), 0)
        col = lax.broadcasted_iota(jnp.int32, (CHUNK, CHUNK), 1)
        causal = col <= row

    for g in range(GM_GROUPS):
        sl = slice(g * GM_GROUP_DIM, (g + 1) * GM_GROUP_DIM)
        vg = v_scr[:, sl] * rinv * vn_ref[:, sl]
        ug = _gelu(_dot(h, win_ref[:, sl]))
        if time_major:
            v_ref[:, sl] = vg
            nb = tm // nt
            for t in range(nt):
                m = ws_ref[(g * nt + t) * nt] * vg[0:nb]
                for s in range(1, t + 1):
                    m = m + ws_ref[(g * nt + t) * nt + s] * vg[s * nb:(s + 1) * nb]
                m = m + bs_ref[g * nt + t]
                out_scr[t * nb:(t + 1) * nb, sl] = (ug[t * nb:(t + 1) * nb] * m).astype(BF16)
        else:
            vb = vg.astype(BF16)
            wm = jnp.where(causal, ws_ref[g], 0.0).astype(BF16)
            for c in range(tm // CHUNK):
                rs = slice(c * CHUNK, (c + 1) * CHUNK)
                m = _dot(wm, vb[rs]) + bias_ref[:, sl]
                out_scr[rs, sl] = (ug[rs] * m).astype(BF16)

    o_ref[...] = x + _dot(out_scr[...], wout_ref[...])


def _gmlp(x, nw, w_in, vn, w_out, *, tm, ws=None, bias=None, ws_flat=None, bs_flat=None, nt=None):
    rows = x.shape[0]
    time_major = ws is None
    in_specs = [_row_spec(tm, D_MODEL), _layer_spec((1, D_MODEL), 0), _layer_spec((D_MODEL, 2 * GM_HALF), 0),
                _layer_spec((1, GM_HALF), 0)]
    out_specs = [_row_spec(tm, D_MODEL)]
    out_shape = [jax.ShapeDtypeStruct((rows, D_MODEL), F32)]
    if time_major:
        args = [x, nw, w_in, vn, ws_flat, bs_flat, w_out]
        in_specs += [_SMEM_SPEC, _SMEM_SPEC]
        out_specs.append(_row_spec(tm, GM_HALF))
        out_shape.append(jax.ShapeDtypeStruct((rows, GM_HALF), F32))
    else:
        args = [x, nw, w_in, vn, ws, bias, w_out]
        in_specs += [_layer_spec((GM_GROUPS, CHUNK, CHUNK), 0), _const_spec((CHUNK, GM_HALF))]
    in_specs.append(_layer_spec((GM_HALF, D_MODEL), 0))
    return pl.pallas_call(
        functools.partial(_gmlp_kernel, tm=tm, time_major=time_major, nt=nt),
        grid=(rows // tm,),
        in_specs=in_specs,
        out_specs=out_specs,
        out_shape=out_shape,
        scratch_shapes=[pltpu.VMEM((tm, GM_HALF), F32), pltpu.VMEM((tm, GM_HALF), BF16)],
        compiler_params=_params(),
        name="gmlp_sample" if time_major else "gmlp_prompt",
    )(*args)


def _ffn_kernel(*refs, tm, shift, prefix, tiles_per_seq, has_state, with_kv):
    refs = list(refs)
    x_ref, p_ref, nf_ref, wg_ref, wu_ref, cw_ref, cb_ref, wd_ref, pn_ref, pg_ref, pp_ref = refs[:11]
    refs = refs[11:]
    if has_state:
        state_ref = refs.pop(0)
    if with_kv:
        kvn_ref, wkv_ref, rope_ref = refs[:3]
        o_ref, conv_ref, k_ref, v_ref, gbuf, act = refs[3:]
    else:
        fn_ref = refs[0]
        o_ref, conv_ref, gbuf, act = refs[1:]

    if has_state:
        gbuf[0:prefix, :] = state_ref[...]
    else:
        @pl.when(pl.program_id(0) % tiles_per_seq == 0)
        def _():
            gbuf[0:prefix, :] = jnp.zeros((prefix, D_FF), F32)

    x = x_ref[...]
    h = _rms(x, nf_ref[...]).astype(BF16)
    for n in range(FF_BLOCKS):
        sl = slice(n * MXU_COLS, (n + 1) * MXU_COLS)
        g = _dot(h, wg_ref[:, sl])
        gbuf[prefix:prefix + tm, sl] = g
        g1 = gbuf[prefix - shift:prefix - shift + tm, sl]
        g2 = gbuf[prefix - 2 * shift:prefix - 2 * shift + tm, sl]
        conv = cb_ref[:, sl] + g2 * cw_ref[0:1, sl]
        conv = conv + g1 * cw_ref[1:2, sl]
        conv = conv + g * cw_ref[2:3, sl]
        up = _dot(h, wu_ref[:, sl])
        act[:, sl] = (_gelu(conv) * up).astype(BF16)

    conv_ref[...] = gbuf[tm:tm + prefix, :]
    if not has_state:
        gbuf[0:prefix, :] = gbuf[tm:tm + prefix, :]

    x = x + _dot(act[...], wd_ref[...])
    gate = jax.nn.sigmoid(_dot(_rms(x, pn_ref[...]).astype(BF16), pg_ref[...]))
    x = x + _dot(p_ref[...].astype(BF16), pp_ref[...]) * gate

    if with_kv:
        o_ref[...] = x
        kv = _dot(_rms(x, kvn_ref[...]).astype(BF16), wkv_ref[...])
        k_ref[...] = _rope(kv[:, 0:KV_DIM], rope_ref[...])
        v_ref[...] = kv[:, KV_DIM:2 * KV_DIM]
    else:
        o_ref[...] = _rms(x, fn_ref[...])


def _ffn(x, p, w, layer, *, tm, seq=None, p_offset=0, time_steps=None, state=None, kv=None, final_norm=None):
    rows = x.shape[0]
    has_state = state is not None
    with_kv = kv is not None
    time_major = time_steps is not None
    if time_major:
        shift = rows // time_steps
        prefix, tiles_per_seq = (CONV_W - 1) * shift, 1
        conv_rows, conv_spec = prefix, pl.BlockSpec((prefix, D_FF), lambda i: (0, 0))
    else:
        shift, prefix = 1, SUBLANES
        tiles_per_seq = seq // tm
        conv_rows = (rows // seq) * prefix
        conv_spec = pl.BlockSpec((prefix, D_FF), lambda i: (i // tiles_per_seq, 0))

    args = [x, p, w["nf"], w["wg"], w["wu"], w["cw"], w["cb"], w["wd"], w["pn"], w["pg"], w["pp"]]
    in_specs = [
        _row_spec(tm, D_MODEL), _row_spec(tm, PLE_DIM, p_offset), _layer_spec((1, D_MODEL), layer),
        _layer_spec((D_MODEL, D_FF), layer), _layer_spec((D_MODEL, D_FF), layer),
        _layer_spec((CONV_W, D_FF), layer), _layer_spec((1, D_FF), layer), _layer_spec((D_FF, D_MODEL), layer),
        _layer_spec((1, D_MODEL), layer), _layer_spec((D_MODEL, D_MODEL), layer),
        _layer_spec((PLE_DIM, D_MODEL), layer),
    ]
    scratch = [pltpu.VMEM((prefix + tm, D_FF), F32), pltpu.VMEM((tm, D_FF), BF16)]
    if has_state:
        args.append(state)
        in_specs.append(_const_spec((prefix, D_FF)))
    out_specs = [_row_spec(tm, D_MODEL), conv_spec]
    out_shape = [jax.ShapeDtypeStruct((rows, D_MODEL), F32), jax.ShapeDtypeStruct((conv_rows, D_FF), F32)]
    if with_kv:
        kvn, wkv, rope = kv
        args += [kvn, wkv, rope]
        rope_tiles = rope.shape[0] // tm
        in_specs += [_const_spec((1, D_MODEL)), _const_spec((D_MODEL, 2 * KV_DIM)),
                     pl.BlockSpec((tm, 2 * LANES), lambda i: (i % rope_tiles, 0))]
        out_specs += [_row_spec(tm, KV_DIM), _row_spec(tm, KV_DIM)]
        out_shape += [jax.ShapeDtypeStruct((rows, KV_DIM), F32)] * 2
    else:
        args.append(final_norm)
        in_specs.append(_const_spec((1, D_MODEL)))

    return pl.pallas_call(
        functools.partial(_ffn_kernel, tm=tm, shift=shift, prefix=prefix, tiles_per_seq=tiles_per_seq,
                          has_state=has_state, with_kv=with_kv),
        grid=(rows // tm,),
        in_specs=in_specs,
        out_specs=out_specs,
        out_shape=out_shape,
        scratch_shapes=scratch,
        compiler_params=_params(),
        name=("ffn_kv" if with_kv else "ffn_final") + ("_sample" if time_major else "_prompt"),
    )(*args)


def _q_proj(x_ref, nm_ref, wq_ref, rope_ref, q_scr):
    h = _rms(x_ref[...], nm_ref[...]).astype(BF16)
    table = rope_ref[...]
    for c in range(D_MODEL // MXU_COLS):
        qc = _dot(h, wq_ref[:, c * MXU_COLS:(c + 1) * MXU_COLS])
        for s in range(MXU_COLS // LANES):
            col = c * MXU_COLS + s * LANES
            qs = _rope(qc[:, s * LANES:(s + 1) * LANES], table) * (HEAD_DIM ** -0.5)
            q_scr[:, col:col + LANES] = qs.astype(q_scr.dtype)


def _head_frames(kv):
    lane = lax.broadcasted_iota(jnp.int32, kv.shape, 1)
    lo = lane < HEAD_DIM
    sw = pltpu.roll(kv, HEAD_DIM, 1)
    zero = jnp.zeros_like(kv)
    return [jnp.where(lo, kv, zero), jnp.where(lo, zero, sw), jnp.where(lo, sw, zero), jnp.where(lo, zero, kv)]


def _softmax_parts(s, mask, sink):
    s = jnp.where(mask, s, NEG_INF)
    m = jnp.maximum(jnp.max(s, axis=-1, keepdims=True), sink)
    e = jnp.exp(s - m)
    den = jnp.sum(e, axis=-1, keepdims=True) + jnp.exp(sink - m)
    return e, 1.0 / den


def _attend(qstack, k_lo, k_hi, v_lo, v_hi, mask, sinks_ref, first_slab, rows):
    weights, scales = [], []
    for par, k_fr in enumerate((k_lo, k_hi)):
        sc = _dot_nt(qstack, k_fr)
        w_par, r_par = [], []
        for s in range(SLABS_PER_KV):
            head = (first_slab + s) * 2 + par
            e, r = _softmax_parts(sc[s * rows:(s + 1) * rows], mask, sinks_ref[head])
            w_par.append(e.astype(BF16))
            r_par.append(r)
        weights.append(w_par)
        scales.append(r_par)
    vcat = jnp.concatenate([v_lo, v_hi], axis=0)
    wcat = jnp.concatenate(
        [jnp.concatenate([weights[0][s], weights[1][s]], axis=1) for s in range(SLABS_PER_KV)], axis=0)
    o = _dot(wcat, vcat)
    lane = lax.broadcasted_iota(jnp.int32, (rows, LANES), 1)
    return [o[s * rows:(s + 1) * rows] * jnp.where(lane < HEAD_DIM, scales[0][s], scales[1][s])
            for s in range(SLABS_PER_KV)]


def _attn_prompt_kernel(x_ref, nm_ref, wq_ref, rope_ref, kc_ref, kp_ref, vc_ref, vp_ref, sinks_ref, wo_ref,
                        o_ref, q_scr, kx, vx, a_scr, *, tm, tiles_per_seq):
    first_tile = pl.program_id(0) % tiles_per_seq == 0
    _q_proj(x_ref, nm_ref, wq_ref, rope_ref, q_scr)

    for f, fr in enumerate(_head_frames(jnp.concatenate([kp_ref[...], kc_ref[...]], axis=0))):
        kx[f] = fr.astype(BF16)
    for f, fr in enumerate(_head_frames(jnp.concatenate([vp_ref[...], vc_ref[...]], axis=0))):
        vx[f] = fr.astype(BF16)

    qi = lax.broadcasted_iota(jnp.int32, (WINDOW, 2 * WINDOW), 0)
    sj = lax.broadcasted_iota(jnp.int32, (WINDOW, 2 * WINDOW), 1)
    band = (sj > qi) & (sj <= qi + WINDOW)

    def q_block(qb, carry):
        r0 = pl.multiple_of(qb * WINDOW, WINDOW)
        lo_lim = jnp.where(jnp.logical_and(first_tile, qb == 0), WINDOW, 0)
        mask = band & (sj >= lo_lim)
        for j in range(N_KV_HEADS):
            first_slab = SLABS_PER_KV * j
            qstack = jnp.concatenate(
                [q_scr[pl.ds(r0, WINDOW), (first_slab + s) * LANES:(first_slab + s + 1) * LANES]
                 for s in range(SLABS_PER_KV)], axis=0)
            keys = pl.ds(r0, 2 * WINDOW)
            outs = _attend(qstack, kx[2 * j, keys, :], kx[2 * j + 1, keys, :], vx[2 * j, keys, :],
                           vx[2 * j + 1, keys, :], mask, sinks_ref, first_slab, WINDOW)
            for s, o in enumerate(outs):
                col = (first_slab + s) * LANES
                a_scr[pl.ds(r0, WINDOW), col:col + LANES] = o.astype(BF16)
        return carry

    lax.fori_loop(0, tm // WINDOW, q_block, 0, unroll=Q_BLOCK_UNROLL)
    o_ref[...] = x_ref[...] + _dot(a_scr[...], wo_ref[...])


def _attn_prompt(x, nm, wq, rope, k, v, sinks, wo, *, tm, seq):
    rows = x.shape[0]
    tiles_per_seq = seq // tm
    blocks_per_tile = tm // WINDOW
    cur = pl.BlockSpec((tm, KV_DIM), lambda i: (i, 0))
    prev = pl.BlockSpec((WINDOW, KV_DIM), lambda i: (jnp.maximum(i * blocks_per_tile - 1, 0), 0))
    return pl.pallas_call(
        functools.partial(_attn_prompt_kernel, tm=tm, tiles_per_seq=tiles_per_seq),
        grid=(rows // tm,),
        in_specs=[
            _row_spec(tm, D_MODEL), _layer_spec((1, D_MODEL), 1), _layer_spec((D_MODEL, D_MODEL), 0),
            pl.BlockSpec((tm, 2 * LANES), lambda i: (i % tiles_per_seq, 0)),
            cur, prev, cur, prev, _SMEM_SPEC, _layer_spec((D_MODEL, D_MODEL), 0),
        ],
        out_specs=_row_spec(tm, D_MODEL),
        out_shape=jax.ShapeDtypeStruct((rows, D_MODEL), F32),
        scratch_shapes=[
            pltpu.VMEM((tm, D_MODEL), BF16),
            pltpu.VMEM((4, WINDOW + tm, LANES), BF16),
            pltpu.VMEM((4, WINDOW + tm, LANES), BF16),
            pltpu.VMEM((tm, D_MODEL), BF16),
        ],
        compiler_params=_params(),
        name="attn_prompt",
    )(x, nm, wq, rope, k, k, v, v, sinks, wo)


def _attn_sample_kernel(x_ref, nm_ref, wq_ref, rope_ref, ck_ref, cv_ref, kn_ref, vn_ref, sinks_ref, wo_ref,
                        o_ref, q_scr, a_scr, *, n_seq, n_new):
    _q_proj(x_ref, nm_ref, wq_ref, rope_ref, q_scr)

    rows = SEQ_GROUP * PAD_T
    n_cached = SEQ_GROUP * WINDOW
    frame = n_cached + LANES
    r = lax.broadcasted_iota(jnp.int32, (rows, frame), 0)
    c = lax.broadcasted_iota(jnp.int32, (rows, frame), 1)
    t_bits, w_bits = PAD_T.bit_length() - 1, WINDOW.bit_length() - 1
    q_seq, q_t = r >> t_bits, r & (PAD_T - 1)
    fresh = c - n_cached
    fresh_t = fresh & (PAD_T - 1)
    seen_cached = ((c >> w_bits) == q_seq) & ((c & (WINDOW - 1)) > q_t)
    seen_fresh = ((fresh >> t_bits) == q_seq) & (fresh_t <= q_t) & (fresh_t < n_new)
    mask = seen_cached | seen_fresh
    fresh_pad = jnp.zeros((LANES - rows, KV_DIM), F32)

    def seq_group(g, carry):
        r0 = pl.multiple_of(g * rows, rows)
        seqs = pl.ds(g * SEQ_GROUP, SEQ_GROUP)
        k_fr = [f.astype(BF16) for f in _head_frames(jnp.concatenate(
            [ck_ref[seqs].reshape(n_cached, KV_DIM), kn_ref[pl.ds(r0, rows), :], fresh_pad], axis=0))]
        v_fr = [f.astype(BF16) for f in _head_frames(jnp.concatenate(
            [cv_ref[seqs].reshape(n_cached, KV_DIM), vn_ref[pl.ds(r0, rows), :], fresh_pad], axis=0))]
        for j in range(N_KV_HEADS):
            first_slab = SLABS_PER_KV * j
            qstack = jnp.concatenate(
                [q_scr[pl.ds(r0, rows), (first_slab + s) * LANES:(first_slab + s + 1) * LANES]
                 for s in range(SLABS_PER_KV)], axis=0)
            outs = _attend(qstack, k_fr[2 * j], k_fr[2 * j + 1], v_fr[2 * j], v_fr[2 * j + 1], mask, sinks_ref,
                           first_slab, rows)
            for s, o in enumerate(outs):
                col = (first_slab + s) * LANES
                a_scr[pl.ds(r0, rows), col:col + LANES] = o.astype(BF16)
        return carry

    lax.fori_loop(0, n_seq // SEQ_GROUP, seq_group, 0)
    o_ref[...] = x_ref[...] + _dot(a_scr[...], wo_ref[...])


def _attn_sample(x, nm, wq, rope, ck, cv, kn, vn, sinks, wo, *, n_seq, n_new):
    rows = x.shape[0]
    full = lambda shape: pl.BlockSpec(shape, lambda i: (0,) * len(shape))
    return pl.pallas_call(
        functools.partial(_attn_sample_kernel, n_seq=n_seq, n_new=n_new),
        grid=(1,),
        in_specs=[
            full((rows, D_MODEL)), _layer_spec((1, D_MODEL), 1), _layer_spec((D_MODEL, D_MODEL), 0),
            full((rows, 2 * LANES)), full(ck.shape), full(cv.shape), full(kn.shape), full(vn.shape),
            _SMEM_SPEC, _layer_spec((D_MODEL, D_MODEL), 0),
        ],
        out_specs=full((rows, D_MODEL)),
        out_shape=jax.ShapeDtypeStruct((rows, D_MODEL), F32),
        scratch_shapes=[pltpu.VMEM((rows, D_MODEL), BF16), pltpu.VMEM((rows, D_MODEL), BF16)],
        compiler_params=_params(),
        name="attn_sample",
    )(x, nm, wq, rope, ck, cv, kn, vn, sinks, wo)


def _rope_table(pos):
    dim = np.arange(LANES) % HEAD_DIM
    inv_freq = ROPE_THETA ** (-jnp.arange(ROT_HALF, dtype=F32) / ROT_HALF)
    ang = pos.astype(F32)[:, None] * inv_freq[dim % ROT_HALF][None, :]
    in_rot = jnp.asarray(dim < ROT_DIM)[None, :]
    first = jnp.asarray(dim < ROT_HALF)[None, :]
    sin = jnp.sin(ang)
    c = jnp.where(in_rot, jnp.cos(ang), 1.0)
    s = jnp.where(first, -sin, jnp.where(in_rot, sin, 0.0))
    return jnp.concatenate([c, s], axis=1)


def kernel(x_prompt, x_sample, state_ffn_conv, cache_k_win, cache_v_win, p_prompt, p_sample, norm_mix, gm_w_in,
           gm_v_norm, gm_w_s, gm_b_s, gm_w_out, kv_norm, w_kv, w_q, attn_sinks, w_o, norm_ffn, ffn_w_gate,
           ffn_w_up, ffn_conv_w, ffn_conv_b, ffn_w_down, ple_norm, ple_w_gate, ple_w_proj, final_norm):
    batch, seq, _ = x_prompt.shape
    dec_batch, dec_seq, _ = x_sample.shape
    depth = norm_mix.shape[0]
    assert depth == 2 and gm_w_in.shape[0] == 1 and w_q.shape[0] == 1
    tm, tm_ffn = 1024, 512
    assert seq % tm == 0 and seq % tm_ffn == 0 and dec_seq <= PAD_T and dec_batch % SEQ_GROUP == 0

    rows3 = lambda a: a.reshape(a.shape[0], 1, a.shape[1])
    ffn_w = dict(
        nf=rows3(norm_ffn), wg=ffn_w_gate.astype(BF16), wu=ffn_w_up.astype(BF16), cw=ffn_conv_w,
        cb=rows3(ffn_conv_b), wd=ffn_w_down.astype(BF16), pn=rows3(ple_norm), pg=ple_w_gate.astype(BF16),
        pp=ple_w_proj.astype(BF16))
    w_in = gm_w_in.astype(BF16)
    w_out = gm_w_out.astype(BF16)
    wq = w_q.astype(BF16)
    wo = w_o.astype(BF16)
    wkv = w_kv.astype(BF16)
    kvn = kv_norm.reshape(1, D_MODEL)
    nm = rows3(norm_mix)
    vn = rows3(gm_v_norm)
    fn = final_norm.reshape(1, D_MODEL)
    sinks = attn_sinks[0]

    n_rows_p = batch * seq
    rope_p = _rope_table(jnp.arange(seq, dtype=jnp.int32))
    bias_p = jnp.repeat(gm_b_s[0].T, GM_GROUP_DIM, axis=1)
    xp = x_prompt.reshape(n_rows_p, D_MODEL)
    p_p = p_prompt.reshape(depth * n_rows_p, PLE_DIM)
    (xp,) = _gmlp(xp, nm, w_in, vn, w_out, tm=tm, ws=gm_w_s, bias=bias_p)
    xp, conv_p0, k_p, v_p = _ffn(xp, p_p, ffn_w, 0, tm=tm_ffn, seq=seq, kv=(kvn, wkv, rope_p))
    xp = _attn_prompt(xp, nm, wq, rope_p, k_p, v_p, sinks, wo, tm=tm, seq=seq)
    y_p, conv_p1 = _ffn(xp, p_p, ffn_w, 1, tm=tm_ffn, seq=seq, p_offset=n_rows_p // tm_ffn, final_norm=fn)

    y_prompt = y_p.reshape(batch, seq, D_MODEL)
    new_conv_prompt = jnp.stack(
        [c.reshape(batch, SUBLANES, D_FF)[:, SUBLANES - (CONV_W - 1):] for c in (conv_p0, conv_p1)], axis=0)
    tail = lambda a: a.reshape(batch, seq, KV_DIM)[:, seq - WINDOW:].reshape(batch, WINDOW, N_KV_HEADS, HEAD_DIM)
    new_k_prompt = tail(k_p)
    new_v_prompt = tail(v_p)

    n_rows = dec_batch * dec_seq
    to_tm = lambda a: jnp.swapaxes(a, 0, 1).reshape(a.shape[0] * a.shape[1], a.shape[2])
    from_tm = lambda a, t: jnp.swapaxes(a.reshape(t, dec_batch, a.shape[-1]), 0, 1)
    pos_s = PAST_LEN + jnp.arange(dec_seq, dtype=jnp.int32)
    rope_s = _rope_table(jnp.repeat(pos_s, dec_batch))
    xs = to_tm(x_sample)
    ws_flat = gm_w_s[0][:, :dec_seq, :dec_seq].reshape(-1)
    bs_flat = gm_b_s[0][:, :dec_seq].reshape(-1)
    xs, gm_v = _gmlp(xs, nm, w_in, vn, w_out, tm=n_rows, ws_flat=ws_flat, bs_flat=bs_flat, nt=dec_seq)
    xs, conv_s0, k_s, v_s = _ffn(xs, to_tm(p_sample[0]), ffn_w, 0, tm=n_rows, time_steps=dec_seq,
                                 state=to_tm(state_ffn_conv[0]), kv=(kvn, wkv, rope_s))

    pad_bm = lambda a: jnp.pad(from_tm(a, dec_seq), ((0, 0), (0, PAD_T - dec_seq), (0, 0))).reshape(
        dec_batch * PAD_T, a.shape[-1])
    rope_a = _rope_table(jnp.tile(PAST_LEN + jnp.arange(PAD_T, dtype=jnp.int32), dec_batch))
    xa = _attn_sample(pad_bm(xs), nm, wq, rope_a,
                      cache_k_win.reshape(dec_batch, WINDOW, KV_DIM), cache_v_win.reshape(dec_batch, WINDOW, KV_DIM),
                      pad_bm(k_s), pad_bm(v_s), sinks, wo, n_seq=dec_batch, n_new=dec_seq)
    xs = to_tm(xa.reshape(dec_batch, PAD_T, D_MODEL)[:, :dec_seq])
    y_s, conv_s1 = _ffn(xs, to_tm(p_sample[1]), ffn_w, 1, tm=n_rows, time_steps=dec_seq,
                        state=to_tm(state_ffn_conv[1]), final_norm=fn)

    y_sample = from_tm(y_s, dec_seq)
    new_gm_v_sample = from_tm(gm_v, dec_seq)[None]
    new_conv_sample = jnp.stack([from_tm(c, CONV_W - 1) for c in (conv_s0, conv_s1)], axis=0)
    new_k_sample = from_tm(k_s, dec_seq).reshape(dec_batch, dec_seq, N_KV_HEADS, HEAD_DIM)
    new_v_sample = from_tm(v_s, dec_seq).reshape(dec_batch, dec_seq, N_KV_HEADS, HEAD_DIM)

    return (y_prompt, y_sample, new_gm_v_sample, new_conv_prompt, new_conv_sample,
            new_k_prompt, new_v_prompt, new_k_sample, new_v_sample)
```

```python
import functools

import numpy as np
import jax
import jax.numpy as jnp
from jax import lax
from jax.experimental import pallas as pl
from jax.experimental.pallas import tpu as pltpu

D_MODEL = 1024
GM_HALF = 2 * D_MODEL
GM_GROUPS = 8
GM_GROUP_DIM = GM_HALF // GM_GROUPS
CHUNK = 128
HEAD_DIM = 64
N_HEADS = D_MODEL // HEAD_DIM
N_KV_HEADS = N_HEADS // 8
KV_DIM = N_KV_HEADS * HEAD_DIM
WINDOW = 128
ROT_DIM = HEAD_DIM // 4
ROT_HALF = ROT_DIM // 2
ROPE_THETA = 500000.0
D_FF = 2816
CONV_W = 3
PLE_DIM = 256
EPS = 1e-6
NEG_INF = -1e30
LOG2_E = 1.4426950408889634
PAST_LEN = 16384

LANES = 128
SUBLANES = 8
MXU_COLS = 256
VMEM_LIMIT_BYTES = 56 * 1024 * 1024
SLABS = D_MODEL // LANES
SLABS_PER_KV = SLABS // N_KV_HEADS
FF_BLOCKS = D_FF // MXU_COLS
PAD_T = SUBLANES
SEQ_GROUP = 8
Q_BLOCK_UNROLL = 4

BF16 = jnp.bfloat16
F32 = jnp.float32


def _rms(x, w):
    ms = jnp.sum(x * x, axis=-1, keepdims=True) * (1.0 / x.shape[-1])
    return x * lax.rsqrt(ms + EPS) * w


def _dot(a, b):
    return jnp.dot(a, b, preferred_element_type=F32)


def _dot_nt(a, b):
    return lax.dot_general(a, b, (((1,), (1,)), ((), ())), preferred_element_type=F32)


def _gelu(x):
    return jax.nn.gelu(x, approximate=True)


def _rope(x, table):
    c = table[:, 0:LANES]
    s = table[:, LANES:2 * LANES]
    lane = lax.broadcasted_iota(jnp.int32, x.shape, 1)
    partner = jnp.where((lane & ROT_HALF) == 0, pltpu.roll(x, LANES - ROT_HALF, 1), pltpu.roll(x, ROT_HALF, 1))
    return x * c + partner * s


def _const_spec(shape):
    zeros = (0,) * len(shape)
    return pl.BlockSpec(shape, lambda i: zeros, pipeline_mode=pl.Buffered(1))


def _layer_spec(shape, layer):
    idx = (layer,) + (0,) * len(shape)
    return pl.BlockSpec((None,) + tuple(shape), lambda i: idx, pipeline_mode=pl.Buffered(1))


def _row_spec(tm, width, offset=0):
    return pl.BlockSpec((tm, width), lambda i: (i + offset, 0))


_SMEM_SPEC = pl.BlockSpec(memory_space=pltpu.SMEM)


def _params():
    return pltpu.CompilerParams(dimension_semantics=("arbitrary",), vmem_limit_bytes=VMEM_LIMIT_BYTES)


def _gmlp_kernel(*refs, tm, time_major, nt=None):
    if time_major:
        x_ref, nw_ref, win_ref, vn_ref, ws_ref, bs_ref, wout_ref, o_ref, v_ref, v_scr, out_scr = refs
    else:
        x_ref, nw_ref, win_ref, vn_ref, ws_ref, bias_ref, wout_ref, o_ref, v_scr, out_scr = refs

    x = x_ref[...]
    h = _rms(x, nw_ref[...]).astype(BF16)

    ssq = jnp.zeros((tm, 1), F32)
    for g in range(GM_GROUPS):
        sl = slice(g * GM_GROUP_DIM, (g + 1) * GM_GROUP_DIM)
        vg = _gelu(_dot(h, win_ref[:, GM_HALF + g * GM_GROUP_DIM:GM_HALF + (g + 1) * GM_GROUP_DIM]))
        v_scr[:, sl] = vg
        ssq = ssq + jnp.sum(vg * vg, axis=-1, keepdims=True)
    rinv = lax.rsqrt(ssq * (1.0 / GM_HALF) + EPS)

    if not time_major:
        row = lax.broadcasted_iota(jnp.int32, (CHUNK, CHUNK), 0)
        col = lax.broadcasted_iota(jnp.int32, (CHUNK, CHUNK), 1)
        causal = col <= row

    for g in range(GM_GROUPS):
        sl = slice(g * GM_GROUP_DIM, (g + 1) * GM_GROUP_DIM)
        vg = v_scr[:, sl] * rinv * vn_ref[:, sl]
        ug = _gelu(_dot(h, win_ref[:, sl]))
        if time_major:
            v_ref[:, sl] = vg
            nb = tm // nt
            for t in range(nt):
                m = ws_ref[(g * nt + t) * nt] * vg[0:nb]
                for s in range(1, t + 1):
                    m = m + ws_ref[(g * nt + t) * nt + s] * vg[s * nb:(s + 1) * nb]
                m = m + bs_ref[g * nt + t]
                out_scr[t * nb:(t + 1) * nb, sl] = (ug[t * nb:(t + 1) * nb] * m).astype(BF16)
        else:
            vb = vg.astype(BF16)
            wm = jnp.where(causal, ws_ref[g], 0.0).astype(BF16)
            for c in range(tm // CHUNK):
                rs = slice(c * CHUNK, (c + 1) * CHUNK)
                m = _dot(wm, vb[rs]) + bias_ref[:, sl]
                out_scr[rs, sl] = (ug[rs] * m).astype(BF16)

    o_ref[...] = x + _dot(out_scr[...], wout_ref[...])


def _gmlp(x, nw, w_in, vn, w_out, *, tm, ws=None, bias=None, ws_flat=None, bs_flat=None, nt=None):
    rows = x.shape[0]
    time_major = ws is None
    in_specs = [_row_spec(tm, D_MODEL), _layer_spec((1, D_MODEL), 0), _layer_spec((D_MODEL, 2 * GM_HALF), 0),
                _layer_spec((1, GM_HALF), 0)]
    out_specs = [_row_spec(tm, D_MODEL)]
    out_shape = [jax.ShapeDtypeStruct((rows, D_MODEL), F32)]
    if time_major:
        args = [x, nw, w_in, vn, ws_flat, bs_flat, w_out]
        in_specs += [_SMEM_SPEC, _SMEM_SPEC]
        out_specs.append(_row_spec(tm, GM_HALF))
        out_shape.append(jax.ShapeDtypeStruct((rows, GM_HALF), F32))
    else:
        args = [x, nw, w_in, vn, ws, bias, w_out]
        in_specs += [_layer_spec((GM_GROUPS, CHUNK, CHUNK), 0), _const_spec((CHUNK, GM_HALF))]
    in_specs.append(_layer_spec((GM_HALF, D_MODEL), 0))
    return pl.pallas_call(
        functools.partial(_gmlp_kernel, tm=tm, time_major=time_major, nt=nt),
        grid=(rows // tm,),
        in_specs=in_specs,
        out_specs=out_specs,
        out_shape=out_shape,
        scratch_shapes=[pltpu.VMEM((tm, GM_HALF), F32), pltpu.VMEM((tm, GM_HALF), BF16)],
        compiler_params=_params(),
        name="gmlp_sample" if time_major else "gmlp_prompt",
    )(*args)


def _ffn_kernel(*refs, tm, shift, prefix, tiles_per_seq, has_state, with_kv):
    refs = list(refs)
    x_ref, p_ref, nf_ref, wg_ref, wu_ref, cw_ref, cb_ref, wd_ref, pn_ref, pg_ref, pp_ref = refs[:11]
    refs = refs[11:]
    if has_state:
        state_ref = refs.pop(0)
    if with_kv:
        kvn_ref, wkv_ref, rope_ref = refs[:3]
        o_ref, conv_ref, kv_ref, gbuf, act = refs[3:]
    else:
        fn_ref = refs[0]
        o_ref, conv_ref, gbuf, act = refs[1:]

    if has_state:
        gbuf[0:prefix, :] = state_ref[...]
    else:
        @pl.when(pl.program_id(0) % tiles_per_seq == 0)
        def _():
            gbuf[0:prefix, :] = jnp.zeros((prefix, D_FF), F32)

    x = x_ref[...]
    h = _rms(x, nf_ref[...]).astype(BF16)
    for n in range(FF_BLOCKS):
        sl = slice(n * MXU_COLS, (n + 1) * MXU_COLS)
        g = _dot(h, wg_ref[:, sl])
        gbuf[prefix:prefix + tm, sl] = g
        g1 = gbuf[prefix - shift:prefix - shift + tm, sl]
        g2 = gbuf[prefix - 2 * shift:prefix - 2 * shift + tm, sl]
        conv = cb_ref[:, sl] + g2 * cw_ref[0:1, sl]
        conv = conv + g1 * cw_ref[1:2, sl]
        conv = conv + g * cw_ref[2:3, sl]
        up = _dot(h, wu_ref[:, sl])
        act[:, sl] = (_gelu(conv) * up).astype(BF16)

    conv_ref[...] = gbuf[tm:tm + prefix, :]
    if not has_state:
        gbuf[0:prefix, :] = gbuf[tm:tm + prefix, :]

    x = x + _dot(act[...], wd_ref[...])
    gate = jax.nn.sigmoid(_dot(_rms(x, pn_ref[...]).astype(BF16), pg_ref[...]))
    x = x + _dot(p_ref[...].astype(BF16), pp_ref[...]) * gate

    if with_kv:
        o_ref[...] = x
        kv = _dot(_rms(x, kvn_ref[...]).astype(BF16), wkv_ref[...])
        kv_ref[:, 0:KV_DIM] = _rope(kv[:, 0:KV_DIM], rope_ref[...])
        kv_ref[:, KV_DIM:2 * KV_DIM] = kv[:, KV_DIM:2 * KV_DIM]
    else:
        o_ref[...] = _rms(x, fn_ref[...])


def _ffn(x, p, w, layer, *, tm, seq=None, p_offset=0, time_steps=None, state=None, kv=None, final_norm=None):
    rows = x.shape[0]
    has_state = state is not None
    with_kv = kv is not None
    time_major = time_steps is not None
    if time_major:
        shift = rows // time_steps
        prefix, tiles_per_seq = (CONV_W - 1) * shift, 1
        conv_rows, conv_spec = prefix, pl.BlockSpec((prefix, D_FF), lambda i: (0, 0))
    else:
        shift, prefix = 1, SUBLANES
        tiles_per_seq = seq // tm
        conv_rows = (rows // seq) * prefix
        conv_spec = pl.BlockSpec((prefix, D_FF), lambda i: (i // tiles_per_seq, 0))

    args = [x, p, w["nf"], w["wg"], w["wu"], w["cw"], w["cb"], w["wd"], w["pn"], w["pg"], w["pp"]]
    in_specs = [
        _row_spec(tm, D_MODEL), _row_spec(tm, PLE_DIM, p_offset), _layer_spec((1, D_MODEL), layer),
        _layer_spec((D_MODEL, D_FF), layer), _layer_spec((D_MODEL, D_FF), layer),
        _layer_spec((CONV_W, D_FF), layer), _layer_spec((1, D_FF), layer), _layer_spec((D_FF, D_MODEL), layer),
        _layer_spec((1, D_MODEL), layer), _layer_spec((D_MODEL, D_MODEL), layer),
        _layer_spec((PLE_DIM, D_MODEL), layer),
    ]
    scratch = [pltpu.VMEM((prefix + tm, D_FF), F32), pltpu.VMEM((tm, D_FF), BF16)]
    if has_state:
        args.append(state)
        in_specs.append(_const_spec((prefix, D_FF)))
    out_specs = [_row_spec(tm, D_MODEL), conv_spec]
    out_shape = [jax.ShapeDtypeStruct((rows, D_MODEL), F32), jax.ShapeDtypeStruct((conv_rows, D_FF), F32)]
    if with_kv:
        kvn, wkv, rope = kv
        args += [kvn, wkv, rope]
        rope_tiles = rope.shape[0] // tm
        in_specs += [_const_spec((1, D_MODEL)), _const_spec((D_MODEL, 2 * KV_DIM)),
                     pl.BlockSpec((tm, 2 * LANES), lambda i: (i % rope_tiles, 0))]
        out_specs.append(_row_spec(tm, 2 * KV_DIM))
        out_shape.append(jax.ShapeDtypeStruct((rows, 2 * KV_DIM), F32))
    else:
        args.append(final_norm)
        in_specs.append(_const_spec((1, D_MODEL)))

    return pl.pallas_call(
        functools.partial(_ffn_kernel, tm=tm, shift=shift, prefix=prefix, tiles_per_seq=tiles_per_seq,
                          has_state=has_state, with_kv=with_kv),
        grid=(rows // tm,),
        in_specs=in_specs,
        out_specs=out_specs,
        out_shape=out_shape,
        scratch_shapes=scratch,
        compiler_params=_params(),
        name=("ffn_kv" if with_kv else "ffn_final") + ("_sample" if time_major else "_prompt"),
    )(*args)


def _q_proj(x_ref, nm_ref, wq_ref, rope_ref, q_scr):
    h = _rms(x_ref[...], nm_ref[...]).astype(BF16)
    table = rope_ref[...]
    for c in range(D_MODEL // MXU_COLS):
        qc = _dot(h, wq_ref[:, c * MXU_COLS:(c + 1) * MXU_COLS])
        for s in range(MXU_COLS // LANES):
            col = c * MXU_COLS + s * LANES
            qs = _rope(qc[:, s * LANES:(s + 1) * LANES], table) * (HEAD_DIM ** -0.5 * LOG2_E)
            q_scr[:, col:col + LANES] = qs.astype(q_scr.dtype)


def _head_frames(kv):
    lane = lax.broadcasted_iota(jnp.int32, kv.shape, 1)
    lo = lane < HEAD_DIM
    sw = pltpu.roll(kv, HEAD_DIM, 1)
    zero = jnp.zeros_like(kv)
    return [jnp.where(lo, kv, zero), jnp.where(lo, zero, sw), jnp.where(lo, sw, zero), jnp.where(lo, zero, kv)]


def _softmax_parts(s, mask, sink):
    s = jnp.where(mask, s, NEG_INF)
    sink = sink * LOG2_E
    m = jnp.maximum(jnp.max(s, axis=-1, keepdims=True), sink)
    e = jnp.exp2(s - m)
    den = jnp.sum(e, axis=-1, keepdims=True) + jnp.exp2(sink - m)
    return e, 1.0 / den


def _attend(qstack, k_lo, k_hi, v_lo, v_hi, mask, sinks_ref, first_slab, rows):
    weights, scales = [], []
    for par, k_fr in enumerate((k_lo, k_hi)):
        sc = _dot_nt(qstack, k_fr)
        w_par, r_par = [], []
        for s in range(SLABS_PER_KV):
            head = (first_slab + s) * 2 + par
            e, r = _softmax_parts(sc[s * rows:(s + 1) * rows], mask, sinks_ref[head])
            w_par.append(e.astype(BF16))
            r_par.append(r)
        weights.append(w_par)
        scales.append(r_par)
    vcat = jnp.concatenate([v_lo, v_hi], axis=0)
    wcat = jnp.concatenate(
        [jnp.concatenate([weights[0][s], weights[1][s]], axis=1) for s in range(SLABS_PER_KV)], axis=0)
    o = _dot(wcat, vcat)
    lane = lax.broadcasted_iota(jnp.int32, (rows, LANES), 1)
    return [o[s * rows:(s + 1) * rows] * jnp.where(lane < HEAD_DIM, scales[0][s], scales[1][s])
            for s in range(SLABS_PER_KV)]


def _attn_prompt_kernel(x_ref, nm_ref, wq_ref, rope_ref, kvc_ref, kvp_ref, sinks_ref, wo_ref,
                        o_ref, q_scr, kx, vx, a_scr, *, tm, tiles_per_seq):
    first_tile = pl.program_id(0) % tiles_per_seq == 0
    _q_proj(x_ref, nm_ref, wq_ref, rope_ref, q_scr)

    kv = jnp.concatenate([kvp_ref[...], kvc_ref[...]], axis=0)
    for f, fr in enumerate(_head_frames(kv[:, 0:KV_DIM])):
        kx[f] = fr.astype(BF16)
    for f, fr in enumerate(_head_frames(kv[:, KV_DIM:2 * KV_DIM])):
        vx[f] = fr.astype(BF16)

    qi = lax.broadcasted_iota(jnp.int32, (WINDOW, 2 * WINDOW), 0)
    sj = lax.broadcasted_iota(jnp.int32, (WINDOW, 2 * WINDOW), 1)
    band = (sj > qi) & (sj <= qi + WINDOW)

    def q_block(qb, carry):
        r0 = pl.multiple_of(qb * WINDOW, WINDOW)
        lo_lim = jnp.where(jnp.logical_and(first_tile, qb == 0), WINDOW, 0)
        mask = band & (sj >= lo_lim)
        for j in range(N_KV_HEADS):
            first_slab = SLABS_PER_KV * j
            qstack = jnp.concatenate(
                [q_scr[pl.ds(r0, WINDOW), (first_slab + s) * LANES:(first_slab + s + 1) * LANES]
                 for s in range(SLABS_PER_KV)], axis=0)
            keys = pl.ds(r0, 2 * WINDOW)
            outs = _attend(qstack, kx[2 * j, keys, :], kx[2 * j + 1, keys, :], vx[2 * j, keys, :],
                           vx[2 * j + 1, keys, :], mask, sinks_ref, first_slab, WINDOW)
            for s, o in enumerate(outs):
                col = (first_slab + s) * LANES
                a_scr[pl.ds(r0, WINDOW), col:col + LANES] = o.astype(BF16)
        return carry

    lax.fori_loop(0, tm // WINDOW, q_block, 0, unroll=Q_BLOCK_UNROLL)
    o_ref[...] = x_ref[...] + _dot(a_scr[...], wo_ref[...])


def _attn_prompt(x, nm, wq, rope, kv, sinks, wo, *, tm, seq):
    rows = x.shape[0]
    tiles_per_seq = seq // tm
    blocks_per_tile = tm // WINDOW
    cur = pl.BlockSpec((tm, 2 * KV_DIM), lambda i: (i, 0))
    prev = pl.BlockSpec((WINDOW, 2 * KV_DIM), lambda i: (jnp.maximum(i * blocks_per_tile - 1, 0), 0))
    return pl.pallas_call(
        functools.partial(_attn_prompt_kernel, tm=tm, tiles_per_seq=tiles_per_seq),
        grid=(rows // tm,),
        in_specs=[
            _row_spec(tm, D_MODEL), _layer_spec((1, D_MODEL), 1), _layer_spec((D_MODEL, D_MODEL), 0),
            pl.BlockSpec((tm, 2 * LANES), lambda i: (i % tiles_per_seq, 0)),
            cur, prev, _SMEM_SPEC, _layer_spec((D_MODEL, D_MODEL), 0),
        ],
        out_specs=_row_spec(tm, D_MODEL),
        out_shape=jax.ShapeDtypeStruct((rows, D_MODEL), F32),
        scratch_shapes=[
            pltpu.VMEM((tm, D_MODEL), BF16),
            pltpu.VMEM((4, WINDOW + tm, LANES), BF16),
            pltpu.VMEM((4, WINDOW + tm, LANES), BF16),
            pltpu.VMEM((tm, D_MODEL), BF16),
        ],
        compiler_params=_params(),
        name="attn_prompt",
    )(x, nm, wq, rope, kv, kv, sinks, wo)


def _attn_sample_kernel(x_ref, nm_ref, wq_ref, rope_ref, ck_ref, cv_ref, kvn_ref, sinks_ref, wo_ref,
                        o_ref, q_scr, a_scr, *, n_seq, n_new):
    _q_proj(x_ref, nm_ref, wq_ref, rope_ref, q_scr)

    rows = SEQ_GROUP * PAD_T
    n_cached = SEQ_GROUP * WINDOW
    frame = n_cached + LANES
    r = lax.broadcasted_iota(jnp.int32, (rows, frame), 0)
    c = lax.broadcasted_iota(jnp.int32, (rows, frame), 1)
    t_bits, w_bits = PAD_T.bit_length() - 1, WINDOW.bit_length() - 1
    q_seq, q_t = r >> t_bits, r & (PAD_T - 1)
    fresh = c - n_cached
    fresh_t = fresh & (PAD_T - 1)
    seen_cached = ((c >> w_bits) == q_seq) & ((c & (WINDOW - 1)) > q_t)
    seen_fresh = ((fresh >> t_bits) == q_seq) & (fresh_t <= q_t) & (fresh_t < n_new)
    mask = seen_cached | seen_fresh
    fresh_pad = jnp.zeros((LANES - rows, KV_DIM), F32)

    def seq_group(g, carry):
        r0 = pl.multiple_of(g * rows, rows)
        seqs = pl.ds(g * SEQ_GROUP, SEQ_GROUP)
        kv_new = kvn_ref[pl.ds(r0, rows), :]
        k_fr = [f.astype(BF16) for f in _head_frames(jnp.concatenate(
            [ck_ref[seqs].reshape(n_cached, KV_DIM), kv_new[:, 0:KV_DIM], fresh_pad], axis=0))]
        v_fr = [f.astype(BF16) for f in _head_frames(jnp.concatenate(
            [cv_ref[seqs].reshape(n_cached, KV_DIM), kv_new[:, KV_DIM:2 * KV_DIM], fresh_pad], axis=0))]
        for j in range(N_KV_HEADS):
            first_slab = SLABS_PER_KV * j
            qstack = jnp.concatenate(
                [q_scr[pl.ds(r0, rows), (first_slab + s) * LANES:(first_slab + s + 1) * LANES]
                 for s in range(SLABS_PER_KV)], axis=0)
            outs = _attend(qstack, k_fr[2 * j], k_fr[2 * j + 1], v_fr[2 * j], v_fr[2 * j + 1], mask, sinks_ref,
                           first_slab, rows)
            for s, o in enumerate(outs):
                col = (first_slab + s) * LANES
                a_scr[pl.ds(r0, rows), col:col + LANES] = o.astype(BF16)
        return carry

    lax.fori_loop(0, n_seq // SEQ_GROUP, seq_group, 0)
    o_ref[...] = x_ref[...] + _dot(a_scr[...], wo_ref[...])


def _attn_sample(x, nm, wq, rope, ck, cv, kvn, sinks, wo, *, n_seq, n_new):
    rows = x.shape[0]
    full = lambda shape: pl.BlockSpec(shape, lambda i: (0,) * len(shape))
    return pl.pallas_call(
        functools.partial(_attn_sample_kernel, n_seq=n_seq, n_new=n_new),
        grid=(1,),
        in_specs=[
            full((rows, D_MODEL)), _layer_spec((1, D_MODEL), 1), _layer_spec((D_MODEL, D_MODEL), 0),
            full((rows, 2 * LANES)), full(ck.shape), full(cv.shape), full(kvn.shape),
            _SMEM_SPEC, _layer_spec((D_MODEL, D_MODEL), 0),
        ],
        out_specs=full((rows, D_MODEL)),
        out_shape=jax.ShapeDtypeStruct((rows, D_MODEL), F32),
        scratch_shapes=[pltpu.VMEM((rows, D_MODEL), BF16), pltpu.VMEM((rows, D_MODEL), BF16)],
        compiler_params=_params(),
        name="attn_sample",
    )(x, nm, wq, rope, ck, cv, kvn, sinks, wo)


def _rope_table(pos):
    dim = np.arange(LANES) % HEAD_DIM
    inv_freq = ROPE_THETA ** (-jnp.arange(ROT_HALF, dtype=F32) / ROT_HALF)
    ang = pos.astype(F32)[:, None] * inv_freq[dim % ROT_HALF][None, :]
    in_rot = jnp.asarray(dim < ROT_DIM)[None, :]
    first = jnp.asarray(dim < ROT_HALF)[None, :]
    sin = jnp.sin(ang)
    c = jnp.where(in_rot, jnp.cos(ang), 1.0)
    s = jnp.where(first, -sin, jnp.where(in_rot, sin, 0.0))
    return jnp.concatenate([c, s], axis=1)


def kernel(x_prompt, x_sample, state_ffn_conv, cache_k_win, cache_v_win, p_prompt, p_sample, norm_mix, gm_w_in,
           gm_v_norm, gm_w_s, gm_b_s, gm_w_out, kv_norm, w_kv, w_q, attn_sinks, w_o, norm_ffn, ffn_w_gate,
           ffn_w_up, ffn_conv_w, ffn_conv_b, ffn_w_down, ple_norm, ple_w_gate, ple_w_proj, final_norm):
    batch, seq, _ = x_prompt.shape
    dec_batch, dec_seq, _ = x_sample.shape
    depth = norm_mix.shape[0]
    assert depth == 2 and gm_w_in.shape[0] == 1 and w_q.shape[0] == 1
    tm, tm_ffn = 1024, 512
    assert seq % tm == 0 and seq % tm_ffn == 0 and dec_seq <= PAD_T and dec_batch % SEQ_GROUP == 0

    rows3 = lambda a: a.reshape(a.shape[0], 1, a.shape[1])
    ffn_w = dict(
        nf=rows3(norm_ffn), wg=ffn_w_gate.astype(BF16), wu=ffn_w_up.astype(BF16), cw=ffn_conv_w,
        cb=rows3(ffn_conv_b), wd=ffn_w_down.astype(BF16), pn=rows3(ple_norm), pg=ple_w_gate.astype(BF16),
        pp=ple_w_proj.astype(BF16))
    w_in = gm_w_in.astype(BF16)
    w_out = gm_w_out.astype(BF16)
    wq = w_q.astype(BF16)
    wo = w_o.astype(BF16)
    wkv = w_kv.astype(BF16)
    kvn = kv_norm.reshape(1, D_MODEL)
    nm = rows3(norm_mix)
    vn = rows3(gm_v_norm)
    fn = final_norm.reshape(1, D_MODEL)
    sinks = attn_sinks[0]

    n_rows_p = batch * seq
    rope_p = _rope_table(jnp.arange(seq, dtype=jnp.int32))
    bias_p = jnp.repeat(gm_b_s[0].T, GM_GROUP_DIM, axis=1)
    xp = x_prompt.reshape(n_rows_p, D_MODEL)
    p_p = p_prompt.reshape(depth * n_rows_p, PLE_DIM)
    (xp,) = _gmlp(xp, nm, w_in, vn, w_out, tm=tm, ws=gm_w_s, bias=bias_p)
    xp, conv_p0, kv_p = _ffn(xp, p_p, ffn_w, 0, tm=tm_ffn, seq=seq, kv=(kvn, wkv, rope_p))
    xp = _attn_prompt(xp, nm, wq, rope_p, kv_p, sinks, wo, tm=tm, seq=seq)
    y_p, conv_p1 = _ffn(xp, p_p, ffn_w, 1, tm=tm_ffn, seq=seq, p_offset=n_rows_p // tm_ffn, final_norm=fn)

    y_prompt = y_p.reshape(batch, seq, D_MODEL)
    new_conv_prompt = jnp.stack(
        [c.reshape(batch, SUBLANES, D_FF)[:, SUBLANES - (CONV_W - 1):] for c in (conv_p0, conv_p1)], axis=0)
    kv_tail = kv_p.reshape(batch, seq, 2 * KV_DIM)[:, seq - WINDOW:]
    new_k_prompt = kv_tail[:, :, :KV_DIM].reshape(batch, WINDOW, N_KV_HEADS, HEAD_DIM)
    new_v_prompt = kv_tail[:, :, KV_DIM:].reshape(batch, WINDOW, N_KV_HEADS, HEAD_DIM)

    n_rows = dec_batch * dec_seq
    to_tm = lambda a: jnp.swapaxes(a, 0, 1).reshape(a.shape[0] * a.shape[1], a.shape[2])
    from_tm = lambda a, t: jnp.swapaxes(a.reshape(t, dec_batch, a.shape[-1]), 0, 1)
    pos_s = PAST_LEN + jnp.arange(dec_seq, dtype=jnp.int32)
    rope_s = _rope_table(jnp.repeat(pos_s, dec_batch))
    xs = to_tm(x_sample)
    ws_flat = gm_w_s[0][:, :dec_seq, :dec_seq].reshape(-1)
    bs_flat = gm_b_s[0][:, :dec_seq].reshape(-1)
    xs, gm_v = _gmlp(xs, nm, w_in, vn, w_out, tm=n_rows, ws_flat=ws_flat, bs_flat=bs_flat, nt=dec_seq)
    xs, conv_s0, kv_s = _ffn(xs, to_tm(p_sample[0]), ffn_w, 0, tm=n_rows, time_steps=dec_seq,
                             state=to_tm(state_ffn_conv[0]), kv=(kvn, wkv, rope_s))

    pad_bm = lambda a: jnp.pad(from_tm(a, dec_seq), ((0, 0), (0, PAD_T - dec_seq), (0, 0))).reshape(
        dec_batch * PAD_T, a.shape[-1])
    rope_a = _rope_table(jnp.tile(PAST_LEN + jnp.arange(PAD_T, dtype=jnp.int32), dec_batch))
    xa = _attn_sample(pad_bm(xs), nm, wq, rope_a,
                      cache_k_win.reshape(dec_batch, WINDOW, KV_DIM), cache_v_win.reshape(dec_batch, WINDOW, KV_DIM),
                      pad_bm(kv_s), sinks, wo, n_seq=dec_batch, n_new=dec_seq)
    xs = to_tm(xa.reshape(dec_batch, PAD_T, D_MODEL)[:, :dec_seq])
    y_s, conv_s1 = _ffn(xs, to_tm(p_sample[1]), ffn_w, 1, tm=n_rows, time_steps=dec_seq,
                        state=to_tm(state_ffn_conv[1]), final_norm=fn)

    y_sample = from_tm(y_s, dec_seq)
    new_gm_v_sample = from_tm(gm_v, dec_seq)[None]
    new_conv_sample = jnp.stack([from_tm(c, CONV_W - 1) for c in (conv_s0, conv_s1)], axis=0)
    kv_bm = from_tm(kv_s, dec_seq)
    new_k_sample = kv_bm[:, :, :KV_DIM].reshape(dec_batch, dec_seq, N_KV_HEADS, HEAD_DIM)
    new_v_sample = kv_bm[:, :, KV_DIM:].reshape(dec_batch, dec_seq, N_KV_HEADS, HEAD_DIM)

    return (y_prompt, y_sample, new_gm_v_sample, new_conv_prompt, new_conv_sample,
            new_k_prompt, new_v_prompt, new_k_sample, new_v_sample)
```

```python
import functools

import numpy as np
import jax
import jax.numpy as jnp
from jax import lax
from jax.experimental import pallas as pl
from jax.experimental.pallas import tpu as pltpu

D_MODEL = 1024
GM_HALF = 2 * D_MODEL
GM_GROUPS = 8
GM_GROUP_DIM = GM_HALF // GM_GROUPS
CHUNK = 128
HEAD_DIM = 64
N_HEADS = D_MODEL // HEAD_DIM
N_KV_HEADS = N_HEADS // 8
KV_DIM = N_KV_HEADS * HEAD_DIM
WINDOW = 128
ROT_DIM = HEAD_DIM // 4
ROT_HALF = ROT_DIM // 2
ROPE_THETA = 500000.0
D_FF = 2816
CONV_W = 3
PLE_DIM = 256
EPS = 1e-6
NEG_INF = -1e30
LOG2_E = 1.4426950408889634
PAST_LEN = 16384

LANES = 128
SUBLANES = 8
MXU_COLS = 256
VMEM_LIMIT_BYTES = 56 * 1024 * 1024
SLABS = D_MODEL // LANES
SLABS_PER_KV = SLABS // N_KV_HEADS
FF_BLOCKS = D_FF // MXU_COLS
PAD_T = SUBLANES
SEQ_GROUP = 8
Q_BLOCK_UNROLL = 4

BF16 = jnp.bfloat16
F32 = jnp.float32


def _rms_factor(x):
    ms = jnp.sum(x * x, axis=-1, keepdims=True) * (1.0 / x.shape[-1])
    return lax.rsqrt(ms + EPS)


def _rms(x, w):
    return x * _rms_factor(x) * w


def _dot(a, b):
    return jnp.dot(a, b, preferred_element_type=F32)


def _dot_nt(a, b):
    return lax.dot_general(a, b, (((1,), (1,)), ((), ())), preferred_element_type=F32)


def _gelu(x):
    return jax.nn.gelu(x, approximate=True)


def _rope(x, table):
    c = table[:, 0:LANES]
    s = table[:, LANES:2 * LANES]
    lane = lax.broadcasted_iota(jnp.int32, x.shape, 1)
    partner = jnp.where((lane & ROT_HALF) == 0, pltpu.roll(x, LANES - ROT_HALF, 1), pltpu.roll(x, ROT_HALF, 1))
    return x * c + partner * s


def _const_spec(shape):
    zeros = (0,) * len(shape)
    return pl.BlockSpec(shape, lambda i: zeros, pipeline_mode=pl.Buffered(1))


def _layer_spec(shape, layer):
    idx = (layer,) + (0,) * len(shape)
    return pl.BlockSpec((None,) + tuple(shape), lambda i: idx, pipeline_mode=pl.Buffered(1))


def _row_spec(tm, width, offset=0):
    return pl.BlockSpec((tm, width), lambda i: (i + offset, 0))


_SMEM_SPEC = pl.BlockSpec(memory_space=pltpu.SMEM)


def _params():
    return pltpu.CompilerParams(dimension_semantics=("arbitrary",), vmem_limit_bytes=VMEM_LIMIT_BYTES)


def _gmlp_kernel(*refs, tm, time_major, nt=None):
    if time_major:
        x_ref, nw_ref, win_ref, vn_ref, ws_ref, bs_ref, wout_ref, o_ref, v_ref, v_scr, out_scr = refs
    else:
        x_ref, nw_ref, win_ref, vn_ref, ws_ref, bias_ref, wout_ref, o_ref, v_scr, out_scr = refs

    x = x_ref[...]
    h = _rms(x, nw_ref[...]).astype(BF16)

    ssq = jnp.zeros((tm, 1), F32)
    for g in range(GM_GROUPS):
        sl = slice(g * GM_GROUP_DIM, (g + 1) * GM_GROUP_DIM)
        vg = _gelu(_dot(h, win_ref[:, GM_HALF + g * GM_GROUP_DIM:GM_HALF + (g + 1) * GM_GROUP_DIM]))
        v_scr[:, sl] = vg
        ssq = ssq + jnp.sum(vg * vg, axis=-1, keepdims=True)
    rinv = lax.rsqrt(ssq * (1.0 / GM_HALF) + EPS)

    if not time_major:
        row = lax.broadcasted_iota(jnp.int32, (CHUNK, CHUNK), 0)
        col = lax.broadcasted_iota(jnp.int32, (CHUNK, CHUNK), 1)
        causal = col <= row

    for g in range(GM_GROUPS):
        sl = slice(g * GM_GROUP_DIM, (g + 1) * GM_GROUP_DIM)
        vg = v_scr[:, sl] * rinv * vn_ref[:, sl]
        ug = _gelu(_dot(h, win_ref[:, sl]))
        if time_major:
            v_ref[:, sl] = vg
            nb = tm // nt
            for t in range(nt):
                m = ws_ref[(g * nt + t) * nt] * vg[0:nb]
                for s in range(1, t + 1):
                    m = m + ws_ref[(g * nt + t) * nt + s] * vg[s * nb:(s + 1) * nb]
                m = m + bs_ref[g * nt + t]
                out_scr[t * nb:(t + 1) * nb, sl] = (ug[t * nb:(t + 1) * nb] * m).astype(BF16)
        else:
            vb = vg.astype(BF16)
            wm = jnp.where(causal, ws_ref[g], 0.0).astype(BF16)
            for c in range(tm // CHUNK):
                rs = slice(c * CHUNK, (c + 1) * CHUNK)
                m = _dot(wm, vb[rs]) + bias_ref[:, sl]
                out_scr[rs, sl] = (ug[rs] * m).astype(BF16)

    o_ref[...] = x + _dot(out_scr[...], wout_ref[...])


def _gmlp(x, nw, w_in, vn, w_out, *, tm, ws=None, bias=None, ws_flat=None, bs_flat=None, nt=None):
    rows = x.shape[0]
    time_major = ws is None
    in_specs = [_row_spec(tm, D_MODEL), _layer_spec((1, D_MODEL), 0), _layer_spec((D_MODEL, 2 * GM_HALF), 0),
                _layer_spec((1, GM_HALF), 0)]
    out_specs = [_row_spec(tm, D_MODEL)]
    out_shape = [jax.ShapeDtypeStruct((rows, D_MODEL), F32)]
    if time_major:
        args = [x, nw, w_in, vn, ws_flat, bs_flat, w_out]
        in_specs += [_SMEM_SPEC, _SMEM_SPEC]
        out_specs.append(_row_spec(tm, GM_HALF))
        out_shape.append(jax.ShapeDtypeStruct((rows, GM_HALF), F32))
    else:
        args = [x, nw, w_in, vn, ws, bias, w_out]
        in_specs += [_layer_spec((GM_GROUPS, CHUNK, CHUNK), 0), _const_spec((CHUNK, GM_HALF))]
    in_specs.append(_layer_spec((GM_HALF, D_MODEL), 0))
    return pl.pallas_call(
        functools.partial(_gmlp_kernel, tm=tm, time_major=time_major, nt=nt),
        grid=(rows // tm,),
        in_specs=in_specs,
        out_specs=out_specs,
        out_shape=out_shape,
        scratch_shapes=[pltpu.VMEM((tm, GM_HALF), F32), pltpu.VMEM((tm, GM_HALF), BF16)],
        compiler_params=_params(),
        name="gmlp_sample" if time_major else "gmlp_prompt",
    )(*args)


def _ffn_kernel(*refs, tm, shift, prefix, tiles_per_seq, has_state, with_kv):
    refs = list(refs)
    x_ref, p_ref, nf_ref, wg_ref, wu_ref, cw_ref, cb_ref, wd_ref, pn_ref, pg_ref, pp_ref = refs[:11]
    refs = refs[11:]
    if has_state:
        state_ref = refs.pop(0)
    if with_kv:
        kvn_ref, wkv_ref, rope_ref = refs[:3]
        o_ref, conv_ref, kv_ref, gbuf, act = refs[3:]
    else:
        fn_ref = refs[0]
        o_ref, conv_ref, gbuf, act = refs[1:]

    if has_state:
        gbuf[0:prefix, :] = state_ref[...]
    else:
        @pl.when(pl.program_id(0) % tiles_per_seq == 0)
        def _():
            gbuf[0:prefix, :] = jnp.zeros((prefix, D_FF), F32)

    x = x_ref[...]
    h = _rms(x, nf_ref[...]).astype(BF16)
    for n in range(FF_BLOCKS):
        sl = slice(n * MXU_COLS, (n + 1) * MXU_COLS)
        g = _dot(h, wg_ref[:, sl])
        gbuf[prefix:prefix + tm, sl] = g
        g1 = gbuf[prefix - shift:prefix - shift + tm, sl]
        g2 = gbuf[prefix - 2 * shift:prefix - 2 * shift + tm, sl]
        conv = cb_ref[:, sl] + g2 * cw_ref[0:1, sl]
        conv = conv + g1 * cw_ref[1:2, sl]
        conv = conv + g * cw_ref[2:3, sl]
        up = _dot(h, wu_ref[:, sl])
        act[:, sl] = (_gelu(conv) * up).astype(BF16)

    conv_ref[...] = gbuf[tm:tm + prefix, :]
    if not has_state:
        gbuf[0:prefix, :] = gbuf[tm:tm + prefix, :]

    x = x + _dot(act[...], wd_ref[...])
    gate = jax.nn.sigmoid(_dot((x * pn_ref[...]).astype(BF16), pg_ref[...]) * _rms_factor(x))
    x = x + _dot(p_ref[...].astype(BF16), pp_ref[...]) * gate

    if with_kv:
        o_ref[...] = x
        kv = _dot((x * kvn_ref[...]).astype(BF16), wkv_ref[...]) * _rms_factor(x)
        kv_ref[:, 0:KV_DIM] = _rope(kv[:, 0:KV_DIM], rope_ref[...])
        kv_ref[:, KV_DIM:2 * KV_DIM] = kv[:, KV_DIM:2 * KV_DIM]
    else:
        o_ref[...] = _rms(x, fn_ref[...])


def _ffn(x, p, w, layer, *, tm, seq=None, p_offset=0, time_steps=None, state=None, kv=None, final_norm=None):
    rows = x.shape[0]
    has_state = state is not None
    with_kv = kv is not None
    time_major = time_steps is not None
    if time_major:
        shift = rows // time_steps
        prefix, tiles_per_seq = (CONV_W - 1) * shift, 1
        conv_rows, conv_spec = prefix, pl.BlockSpec((prefix, D_FF), lambda i: (0, 0))
    else:
        shift, prefix = 1, SUBLANES
        tiles_per_seq = seq // tm
        conv_rows = (rows // seq) * prefix
        conv_spec = pl.BlockSpec((prefix, D_FF), lambda i: (i // tiles_per_seq, 0))

    args = [x, p, w["nf"], w["wg"], w["wu"], w["cw"], w["cb"], w["wd"], w["pn"], w["pg"], w["pp"]]
    in_specs = [
        _row_spec(tm, D_MODEL), _row_spec(tm, PLE_DIM, p_offset), _layer_spec((1, D_MODEL), layer),
        _layer_spec((D_MODEL, D_FF), layer), _layer_spec((D_MODEL, D_FF), layer),
        _layer_spec((CONV_W, D_FF), layer), _layer_spec((1, D_FF), layer), _layer_spec((D_FF, D_MODEL), layer),
        _layer_spec((1, D_MODEL), layer), _layer_spec((D_MODEL, D_MODEL), layer),
        _layer_spec((PLE_DIM, D_MODEL), layer),
    ]
    scratch = [pltpu.VMEM((prefix + tm, D_FF), F32), pltpu.VMEM((tm, D_FF), BF16)]
    if has_state:
        args.append(state)
        in_specs.append(_const_spec((prefix, D_FF)))
    out_specs = [_row_spec(tm, D_MODEL), conv_spec]
    out_shape = [jax.ShapeDtypeStruct((rows, D_MODEL), F32), jax.ShapeDtypeStruct((conv_rows, D_FF), F32)]
    if with_kv:
        kvn, wkv, rope = kv
        args += [kvn, wkv, rope]
        rope_tiles = rope.shape[0] // tm
        in_specs += [_const_spec((1, D_MODEL)), _const_spec((D_MODEL, 2 * KV_DIM)),
                     pl.BlockSpec((tm, 2 * LANES), lambda i: (i % rope_tiles, 0))]
        out_specs.append(_row_spec(tm, 2 * KV_DIM))
        out_shape.append(jax.ShapeDtypeStruct((rows, 2 * KV_DIM), F32))
    else:
        args.append(final_norm)
        in_specs.append(_const_spec((1, D_MODEL)))

    return pl.pallas_call(
        functools.partial(_ffn_kernel, tm=tm, shift=shift, prefix=prefix, tiles_per_seq=tiles_per_seq,
                          has_state=has_state, with_kv=with_kv),
        grid=(rows // tm,),
        in_specs=in_specs,
        out_specs=out_specs,
        out_shape=out_shape,
        scratch_shapes=scratch,
        compiler_params=_params(),
        name=("ffn_kv" if with_kv else "ffn_final") + ("_sample" if time_major else "_prompt"),
    )(*args)


def _q_proj(x_ref, nm_ref, wq_ref, rope_ref, q_scr):
    h = _rms(x_ref[...], nm_ref[...]).astype(BF16)
    table = rope_ref[...]
    for c in range(D_MODEL // MXU_COLS):
        qc = _dot(h, wq_ref[:, c * MXU_COLS:(c + 1) * MXU_COLS])
        for s in range(MXU_COLS // LANES):
            col = c * MXU_COLS + s * LANES
            qs = _rope(qc[:, s * LANES:(s + 1) * LANES], table) * (HEAD_DIM ** -0.5 * LOG2_E)
            q_scr[:, col:col + LANES] = qs.astype(q_scr.dtype)


def _head_frames(kv):
    lane = lax.broadcasted_iota(jnp.int32, kv.shape, 1)
    lo = lane < HEAD_DIM
    sw = pltpu.roll(kv, HEAD_DIM, 1)
    zero = jnp.zeros_like(kv)
    return [jnp.where(lo, kv, zero), jnp.where(lo, zero, sw), jnp.where(lo, sw, zero), jnp.where(lo, zero, kv)]


def _softmax_parts(s, mask, sink):
    s = jnp.where(mask, s, NEG_INF)
    sink = sink * LOG2_E
    m = jnp.maximum(jnp.max(s, axis=-1, keepdims=True), sink)
    e = jnp.exp2(s - m)
    den = jnp.sum(e, axis=-1, keepdims=True) + jnp.exp2(sink - m)
    return e, 1.0 / den


def _attend(qstack, k_lo, k_hi, v_lo, v_hi, mask, sinks_ref, first_slab, rows):
    weights, scales = [], []
    for par, k_fr in enumerate((k_lo, k_hi)):
        sc = _dot_nt(qstack, k_fr)
        w_par, r_par = [], []
        for s in range(SLABS_PER_KV):
            head = (first_slab + s) * 2 + par
            e, r = _softmax_parts(sc[s * rows:(s + 1) * rows], mask, sinks_ref[head])
            w_par.append(e.astype(BF16))
            r_par.append(r)
        weights.append(w_par)
        scales.append(r_par)
    vcat = jnp.concatenate([v_lo, v_hi], axis=0)
    wcat = jnp.concatenate(
        [jnp.concatenate([weights[0][s], weights[1][s]], axis=1) for s in range(SLABS_PER_KV)], axis=0)
    o = _dot(wcat, vcat)
    lane = lax.broadcasted_iota(jnp.int32, (rows, LANES), 1)
    return [o[s * rows:(s + 1) * rows] * jnp.where(lane < HEAD_DIM, scales[0][s], scales[1][s])
            for s in range(SLABS_PER_KV)]


def _attn_prompt_kernel(x_ref, nm_ref, wq_ref, rope_ref, kvc_ref, kvp_ref, sinks_ref, wo_ref,
                        o_ref, q_scr, kx, vx, a_scr, *, tm, tiles_per_seq):
    first_tile = pl.program_id(0) % tiles_per_seq == 0
    _q_proj(x_ref, nm_ref, wq_ref, rope_ref, q_scr)

    kv = jnp.concatenate([kvp_ref[...], kvc_ref[...]], axis=0)
    for f, fr in enumerate(_head_frames(kv[:, 0:KV_DIM])):
        kx[f] = fr.astype(BF16)
    for f, fr in enumerate(_head_frames(kv[:, KV_DIM:2 * KV_DIM])):
        vx[f] = fr.astype(BF16)

    qi = lax.broadcasted_iota(jnp.int32, (WINDOW, 2 * WINDOW), 0)
    sj = lax.broadcasted_iota(jnp.int32, (WINDOW, 2 * WINDOW), 1)
    band = (sj > qi) & (sj <= qi + WINDOW)

    def q_block(qb, carry):
        r0 = pl.multiple_of(qb * WINDOW, WINDOW)
        lo_lim = jnp.where(jnp.logical_and(first_tile, qb == 0), WINDOW, 0)
        mask = band & (sj >= lo_lim)
        for j in range(N_KV_HEADS):
            first_slab = SLABS_PER_KV * j
            qstack = jnp.concatenate(
                [q_scr[pl.ds(r0, WINDOW), (first_slab + s) * LANES:(first_slab + s + 1) * LANES]
                 for s in range(SLABS_PER_KV)], axis=0)
            keys = pl.ds(r0, 2 * WINDOW)
            outs = _attend(qstack, kx[2 * j, keys, :], kx[2 * j + 1, keys, :], vx[2 * j, keys, :],
                           vx[2 * j + 1, keys, :], mask, sinks_ref, first_slab, WINDOW)
            for s, o in enumerate(outs):
                col = (first_slab + s) * LANES
                a_scr[pl.ds(r0, WINDOW), col:col + LANES] = o.astype(BF16)
        return carry

    lax.fori_loop(0, tm // WINDOW, q_block, 0, unroll=Q_BLOCK_UNROLL)
    o_ref[...] = x_ref[...] + _dot(a_scr[...], wo_ref[...])


def _attn_prompt(x, nm, wq, rope, kv, sinks, wo, *, tm, seq):
    rows = x.shape[0]
    tiles_per_seq = seq // tm
    blocks_per_tile = tm // WINDOW
    cur = pl.BlockSpec((tm, 2 * KV_DIM), lambda i: (i, 0))
    prev = pl.BlockSpec((WINDOW, 2 * KV_DIM), lambda i: (jnp.maximum(i * blocks_per_tile - 1, 0), 0))
    return pl.pallas_call(
        functools.partial(_attn_prompt_kernel, tm=tm, tiles_per_seq=tiles_per_seq),
        grid=(rows // tm,),
        in_specs=[
            _row_spec(tm, D_MODEL), _layer_spec((1, D_MODEL), 1), _layer_spec((D_MODEL, D_MODEL), 0),
            pl.BlockSpec((tm, 2 * LANES), lambda i: (i % tiles_per_seq, 0)),
            cur, prev, _SMEM_SPEC, _layer_spec((D_MODEL, D_MODEL), 0),
        ],
        out_specs=_row_spec(tm, D_MODEL),
        out_shape=jax.ShapeDtypeStruct((rows, D_MODEL), F32),
        scratch_shapes=[
            pltpu.VMEM((tm, D_MODEL), BF16),
            pltpu.VMEM((4, WINDOW + tm, LANES), BF16),
            pltpu.VMEM((4, WINDOW + tm, LANES), BF16),
            pltpu.VMEM((tm, D_MODEL), BF16),
        ],
        compiler_params=_params(),
        name="attn_prompt",
    )(x, nm, wq, rope, kv, kv, sinks, wo)


def _attn_sample_kernel(x_ref, nm_ref, wq_ref, rope_ref, ck_ref, cv_ref, kvn_ref, sinks_ref, wo_ref,
                        o_ref, q_scr, a_scr, *, n_seq, n_new):
    _q_proj(x_ref, nm_ref, wq_ref, rope_ref, q_scr)

    rows = SEQ_GROUP * PAD_T
    n_cached = SEQ_GROUP * WINDOW
    frame = n_cached + LANES
    r = lax.broadcasted_iota(jnp.int32, (rows, frame), 0)
    c = lax.broadcasted_iota(jnp.int32, (rows, frame), 1)
    t_bits, w_bits = PAD_T.bit_length() - 1, WINDOW.bit_length() - 1
    q_seq, q_t = r >> t_bits, r & (PAD_T - 1)
    fresh = c - n_cached
    fresh_t = fresh & (PAD_T - 1)
    seen_cached = ((c >> w_bits) == q_seq) & ((c & (WINDOW - 1)) > q_t)
    seen_fresh = ((fresh >> t_bits) == q_seq) & (fresh_t <= q_t) & (fresh_t < n_new)
    mask = seen_cached | seen_fresh
    fresh_pad = jnp.zeros((LANES - rows, KV_DIM), F32)

    def seq_group(g, carry):
        r0 = pl.multiple_of(g * rows, rows)
        seqs = pl.ds(g * SEQ_GROUP, SEQ_GROUP)
        kv_new = kvn_ref[pl.ds(r0, rows), :]
        k_fr = [f.astype(BF16) for f in _head_frames(jnp.concatenate(
            [ck_ref[seqs].reshape(n_cached, KV_DIM), kv_new[:, 0:KV_DIM], fresh_pad], axis=0))]
        v_fr = [f.astype(BF16) for f in _head_frames(jnp.concatenate(
            [cv_ref[seqs].reshape(n_cached, KV_DIM), kv_new[:, KV_DIM:2 * KV_DIM], fresh_pad], axis=0))]
        for j in range(N_KV_HEADS):
            first_slab = SLABS_PER_KV * j
            qstack = jnp.concatenate(
                [q_scr[pl.ds(r0, rows), (first_slab + s) * LANES:(first_slab + s + 1) * LANES]
                 for s in range(SLABS_PER_KV)], axis=0)
            outs = _attend(qstack, k_fr[2 * j], k_fr[2 * j + 1], v_fr[2 * j], v_fr[2 * j + 1], mask, sinks_ref,
                           first_slab, rows)
            for s, o in enumerate(outs):
                col = (first_slab + s) * LANES
                a_scr[pl.ds(r0, rows), col:col + LANES] = o.astype(BF16)
        return carry

    lax.fori_loop(0, n_seq // SEQ_GROUP, seq_group, 0)
    o_ref[...] = x_ref[...] + _dot(a_scr[...], wo_ref[...])


def _attn_sample(x, nm, wq, rope, ck, cv, kvn, sinks, wo, *, n_seq, n_new):
    rows = x.shape[0]
    full = lambda shape: pl.BlockSpec(shape, lambda i: (0,) * len(shape))
    return pl.pallas_call(
        functools.partial(_attn_sample_kernel, n_seq=n_seq, n_new=n_new),
        grid=(1,),
        in_specs=[
            full((rows, D_MODEL)), _layer_spec((1, D_MODEL), 1), _layer_spec((D_MODEL, D_MODEL), 0),
            full((rows, 2 * LANES)), full(ck.shape), full(cv.shape), full(kvn.shape),
            _SMEM_SPEC, _layer_spec((D_MODEL, D_MODEL), 0),
        ],
        out_specs=full((rows, D_MODEL)),
        out_shape=jax.ShapeDtypeStruct((rows, D_MODEL), F32),
        scratch_shapes=[pltpu.VMEM((rows, D_MODEL), BF16), pltpu.VMEM((rows, D_MODEL), BF16)],
        compiler_params=_params(),
        name="attn_sample",
    )(x, nm, wq, rope, ck, cv, kvn, sinks, wo)


def _rope_table(pos):
    dim = np.arange(LANES) % HEAD_DIM
    inv_freq = ROPE_THETA ** (-jnp.arange(ROT_HALF, dtype=F32) / ROT_HALF)
    ang = pos.astype(F32)[:, None] * inv_freq[dim % ROT_HALF][None, :]
    in_rot = jnp.asarray(dim < ROT_DIM)[None, :]
    first = jnp.asarray(dim < ROT_HALF)[None, :]
    sin = jnp.sin(ang)
    c = jnp.where(in_rot, jnp.cos(ang), 1.0)
    s = jnp.where(first, -sin, jnp.where(in_rot, sin, 0.0))
    return jnp.concatenate([c, s], axis=1)


def kernel(x_prompt, x_sample, state_ffn_conv, cache_k_win, cache_v_win, p_prompt, p_sample, norm_mix, gm_w_in,
           gm_v_norm, gm_w_s, gm_b_s, gm_w_out, kv_norm, w_kv, w_q, attn_sinks, w_o, norm_ffn, ffn_w_gate,
           ffn_w_up, ffn_conv_w, ffn_conv_b, ffn_w_down, ple_norm, ple_w_gate, ple_w_proj, final_norm):
    batch, seq, _ = x_prompt.shape
    dec_batch, dec_seq, _ = x_sample.shape
    depth = norm_mix.shape[0]
    assert depth == 2 and gm_w_in.shape[0] == 1 and w_q.shape[0] == 1
    tm, tm_ffn = 1024, 512
    assert seq % tm == 0 and seq % tm_ffn == 0 and dec_seq <= PAD_T and dec_batch % SEQ_GROUP == 0

    rows3 = lambda a: a.reshape(a.shape[0], 1, a.shape[1])
    ffn_w = dict(
        nf=rows3(norm_ffn), wg=ffn_w_gate.astype(BF16), wu=ffn_w_up.astype(BF16), cw=ffn_conv_w,
        cb=rows3(ffn_conv_b), wd=ffn_w_down.astype(BF16), pn=rows3(ple_norm), pg=ple_w_gate.astype(BF16),
        pp=ple_w_proj.astype(BF16))
    w_in = gm_w_in.astype(BF16)
    w_out = gm_w_out.astype(BF16)
    wq = w_q.astype(BF16)
    wo = w_o.astype(BF16)
    wkv = w_kv.astype(BF16)
    kvn = kv_norm.reshape(1, D_MODEL)
    nm = rows3(norm_mix)
    vn = rows3(gm_v_norm)
    fn = final_norm.reshape(1, D_MODEL)
    sinks = attn_sinks[0]

    n_rows_p = batch * seq
    rope_p = _rope_table(jnp.arange(seq, dtype=jnp.int32))
    bias_p = jnp.repeat(gm_b_s[0].T, GM_GROUP_DIM, axis=1)
    xp = x_prompt.reshape(n_rows_p, D_MODEL)
    p_p = p_prompt.reshape(depth * n_rows_p, PLE_DIM)
    (xp,) = _gmlp(xp, nm, w_in, vn, w_out, tm=tm, ws=gm_w_s, bias=bias_p)
    xp, conv_p0, kv_p = _ffn(xp, p_p, ffn_w, 0, tm=tm_ffn, seq=seq, kv=(kvn, wkv, rope_p))
    xp = _attn_prompt(xp, nm, wq, rope_p, kv_p, sinks, wo, tm=tm, seq=seq)
    y_p, conv_p1 = _ffn(xp, p_p, ffn_w, 1, tm=tm_ffn, seq=seq, p_offset=n_rows_p // tm_ffn, final_norm=fn)

    y_prompt = y_p.reshape(batch, seq, D_MODEL)
    new_conv_prompt = jnp.stack(
        [c.reshape(batch, SUBLANES, D_FF)[:, SUBLANES - (CONV_W - 1):] for c in (conv_p0, conv_p1)], axis=0)
    kv_tail = kv_p.reshape(batch, seq, 2 * KV_DIM)[:, seq - WINDOW:]
    new_k_prompt = kv_tail[:, :, :KV_DIM].reshape(batch, WINDOW, N_KV_HEADS, HEAD_DIM)
    new_v_prompt = kv_tail[:, :, KV_DIM:].reshape(batch, WINDOW, N_KV_HEADS, HEAD_DIM)

    n_rows = dec_batch * dec_seq
    to_tm = lambda a: jnp.swapaxes(a, 0, 1).reshape(a.shape[0] * a.shape[1], a.shape[2])
    from_tm = lambda a, t: jnp.swapaxes(a.reshape(t, dec_batch, a.shape[-1]), 0, 1)
    pos_s = PAST_LEN + jnp.arange(dec_seq, dtype=jnp.int32)
    rope_s = _rope_table(jnp.repeat(pos_s, dec_batch))
    xs = to_tm(x_sample)
    ws_flat = gm_w_s[0][:, :dec_seq, :dec_seq].reshape(-1)
    bs_flat = gm_b_s[0][:, :dec_seq].reshape(-1)
    xs, gm_v = _gmlp(xs, nm, w_in, vn, w_out, tm=n_rows, ws_flat=ws_flat, bs_flat=bs_flat, nt=dec_seq)
    xs, conv_s0, kv_s = _ffn(xs, to_tm(p_sample[0]), ffn_w, 0, tm=n_rows, time_steps=dec_seq,
                             state=to_tm(state_ffn_conv[0]), kv=(kvn, wkv, rope_s))

    pad_bm = lambda a: jnp.pad(from_tm(a, dec_seq), ((0, 0), (0, PAD_T - dec_seq), (0, 0))).reshape(
        dec_batch * PAD_T, a.shape[-1])
    rope_a = _rope_table(jnp.tile(PAST_LEN + jnp.arange(PAD_T, dtype=jnp.int32), dec_batch))
    xa = _attn_sample(pad_bm(xs), nm, wq, rope_a,
                      cache_k_win.reshape(dec_batch, WINDOW, KV_DIM), cache_v_win.reshape(dec_batch, WINDOW, KV_DIM),
                      pad_bm(kv_s), sinks, wo, n_seq=dec_batch, n_new=dec_seq)
    xs = to_tm(xa.reshape(dec_batch, PAD_T, D_MODEL)[:, :dec_seq])
    y_s, conv_s1 = _ffn(xs, to_tm(p_sample[1]), ffn_w, 1, tm=n_rows, time_steps=dec_seq,
                        state=to_tm(state_ffn_conv[1]), final_norm=fn)

    y_sample = from_tm(y_s, dec_seq)
    new_gm_v_sample = from_tm(gm_v, dec_seq)[None]
    new_conv_sample = jnp.stack([from_tm(c, CONV_W - 1) for c in (conv_s0, conv_s1)], axis=0)
    kv_bm = from_tm(kv_s, dec_seq)
    new_k_sample = kv_bm[:, :, :KV_DIM].reshape(dec_batch, dec_seq, N_KV_HEADS, HEAD_DIM)
    new_v_sample = kv_bm[:, :, KV_DIM:].reshape(dec_batch, dec_seq, N_KV_HEADS, HEAD_DIM)

    return (y_prompt, y_sample, new_gm_v_sample, new_conv_prompt, new_conv_sample,
            new_k_prompt, new_v_prompt, new_k_sample, new_v_sample)
```

```python
import functools

import numpy as np
import jax
import jax.numpy as jnp
from jax import lax
from jax.experimental import pallas as pl
from jax.experimental.pallas import tpu as pltpu

D_MODEL = 1024
GM_HALF = 2 * D_MODEL
GM_GROUPS = 8
GM_GROUP_DIM = GM_HALF // GM_GROUPS
CHUNK = 128
HEAD_DIM = 64
N_HEADS = D_MODEL // HEAD_DIM
N_KV_HEADS = N_HEADS // 8
KV_DIM = N_KV_HEADS * HEAD_DIM
WINDOW = 128
ROT_DIM = HEAD_DIM // 4
ROT_HALF = ROT_DIM // 2
ROPE_THETA = 500000.0
D_FF = 2816
CONV_W = 3
PLE_DIM = 256
EPS = 1e-6
NEG_INF = -1e30
LOG2_E = 1.4426950408889634
PAST_LEN = 16384

LANES = 128
SUBLANES = 8
MXU_COLS = 256
VMEM_LIMIT_BYTES = 56 * 1024 * 1024
SLABS = D_MODEL // LANES
SLABS_PER_KV = SLABS // N_KV_HEADS
FF_BLOCKS = D_FF // MXU_COLS
PAD_T = SUBLANES
SEQ_GROUP = 8
Q_BLOCK_UNROLL = 4

BF16 = jnp.bfloat16
F32 = jnp.float32


def _rms_factor(x):
    ms = jnp.sum(x * x, axis=-1, keepdims=True) * (1.0 / x.shape[-1])
    return lax.rsqrt(ms + EPS)


def _rms(x, w):
    return x * _rms_factor(x) * w


def _dot(a, b):
    return jnp.dot(a, b, preferred_element_type=F32)


def _dot_nt(a, b):
    return lax.dot_general(a, b, (((1,), (1,)), ((), ())), preferred_element_type=F32)


def _gelu(x):
    return jax.nn.gelu(x, approximate=True)


def _rope(x, table):
    c = table[:, 0:LANES]
    s = table[:, LANES:2 * LANES]
    lane = lax.broadcasted_iota(jnp.int32, x.shape, 1)
    partner = jnp.where((lane & ROT_HALF) == 0, pltpu.roll(x, LANES - ROT_HALF, 1), pltpu.roll(x, ROT_HALF, 1))
    return x * c + partner * s


def _const_spec(shape):
    zeros = (0,) * len(shape)
    return pl.BlockSpec(shape, lambda i: zeros, pipeline_mode=pl.Buffered(1))


def _layer_spec(shape, layer):
    idx = (layer,) + (0,) * len(shape)
    return pl.BlockSpec((None,) + tuple(shape), lambda i: idx, pipeline_mode=pl.Buffered(1))


def _row_spec(tm, width, offset=0):
    return pl.BlockSpec((tm, width), lambda i: (i + offset, 0))


_SMEM_SPEC = pl.BlockSpec(memory_space=pltpu.SMEM)


def _params():
    return pltpu.CompilerParams(dimension_semantics=("arbitrary",), vmem_limit_bytes=VMEM_LIMIT_BYTES)


def _gmlp_kernel(*refs, tm, time_major, nt=None):
    if time_major:
        x_ref, nw_ref, win_ref, vn_ref, ws_ref, bs_ref, wout_ref, o_ref, v_ref, v_scr, out_scr = refs
    else:
        x_ref, nw_ref, win_ref, vn_ref, ws_ref, bias_ref, wout_ref, o_ref, v_scr, out_scr = refs

    x = x_ref[...]
    h = _rms(x, nw_ref[...]).astype(BF16)

    ssq = jnp.zeros((tm, 1), F32)
    for g in range(GM_GROUPS):
        sl = slice(g * GM_GROUP_DIM, (g + 1) * GM_GROUP_DIM)
        vg = _gelu(_dot(h, win_ref[:, GM_HALF + g * GM_GROUP_DIM:GM_HALF + (g + 1) * GM_GROUP_DIM]))
        ssq = ssq + jnp.sum(vg * vg, axis=-1, keepdims=True)
        if time_major:
            v_scr[:, sl] = vg
        else:
            v_scr[:, sl] = (vg * vn_ref[:, sl]).astype(BF16)
    rinv = lax.rsqrt(ssq * (1.0 / GM_HALF) + EPS)

    if not time_major:
        row = lax.broadcasted_iota(jnp.int32, (CHUNK, CHUNK), 0)
        col = lax.broadcasted_iota(jnp.int32, (CHUNK, CHUNK), 1)
        causal = col <= row
        rinv_cols = [jnp.broadcast_to(rinv[c * CHUNK:(c + 1) * CHUNK], (CHUNK, CHUNK)).T
                     for c in range(tm // CHUNK)]

    for g in range(GM_GROUPS):
        sl = slice(g * GM_GROUP_DIM, (g + 1) * GM_GROUP_DIM)
        ug = _gelu(_dot(h, win_ref[:, sl]))
        if time_major:
            vg = v_scr[:, sl] * rinv * vn_ref[:, sl]
            v_ref[:, sl] = vg
            nb = tm // nt
            for t in range(nt):
                m = ws_ref[(g * nt + t) * nt] * vg[0:nb]
                for s in range(1, t + 1):
                    m = m + ws_ref[(g * nt + t) * nt + s] * vg[s * nb:(s + 1) * nb]
                m = m + bs_ref[g * nt + t]
                out_scr[t * nb:(t + 1) * nb, sl] = (ug[t * nb:(t + 1) * nb] * m).astype(BF16)
        else:
            wm = jnp.where(causal, ws_ref[g], 0.0)
            zero = jnp.zeros((CHUNK, CHUNK), BF16)
            bias2 = jnp.concatenate([bias_ref[:, sl], bias_ref[:, sl]], axis=0)
            for c in range(0, tm // CHUNK, 2):
                rs = slice(c * CHUNK, (c + 2) * CHUNK)
                wa = (wm * rinv_cols[c]).astype(BF16)
                wb = (wm * rinv_cols[c + 1]).astype(BF16)
                wm2 = jnp.concatenate(
                    [jnp.concatenate([wa, zero], axis=1), jnp.concatenate([zero, wb], axis=1)], axis=0)
                m = _dot(wm2, v_scr[rs, sl]) + bias2
                out_scr[rs, sl] = (ug[rs] * m).astype(BF16)

    o_ref[...] = x + _dot(out_scr[...], wout_ref[...])


def _gmlp(x, nw, w_in, vn, w_out, *, tm, ws=None, bias=None, ws_flat=None, bs_flat=None, nt=None):
    rows = x.shape[0]
    time_major = ws is None
    in_specs = [_row_spec(tm, D_MODEL), _layer_spec((1, D_MODEL), 0), _layer_spec((D_MODEL, 2 * GM_HALF), 0),
                _layer_spec((1, GM_HALF), 0)]
    out_specs = [_row_spec(tm, D_MODEL)]
    out_shape = [jax.ShapeDtypeStruct((rows, D_MODEL), F32)]
    if time_major:
        args = [x, nw, w_in, vn, ws_flat, bs_flat, w_out]
        in_specs += [_SMEM_SPEC, _SMEM_SPEC]
        out_specs.append(_row_spec(tm, GM_HALF))
        out_shape.append(jax.ShapeDtypeStruct((rows, GM_HALF), F32))
    else:
        args = [x, nw, w_in, vn, ws, bias, w_out]
        in_specs += [_layer_spec((GM_GROUPS, CHUNK, CHUNK), 0), _const_spec((CHUNK, GM_HALF))]
    in_specs.append(_layer_spec((GM_HALF, D_MODEL), 0))
    return pl.pallas_call(
        functools.partial(_gmlp_kernel, tm=tm, time_major=time_major, nt=nt),
        grid=(rows // tm,),
        in_specs=in_specs,
        out_specs=out_specs,
        out_shape=out_shape,
        scratch_shapes=[pltpu.VMEM((tm, GM_HALF), F32 if time_major else BF16), pltpu.VMEM((tm, GM_HALF), BF16)],
        compiler_params=_params(),
        name="gmlp_sample" if time_major else "gmlp_prompt",
    )(*args)


def _ffn_kernel(*refs, tm, shift, prefix, tiles_per_seq, has_state, with_kv):
    refs = list(refs)
    x_ref, p_ref, nf_ref, wg_ref, wu_ref, cw_ref, cb_ref, wd_ref, pn_ref, pg_ref, pp_ref = refs[:11]
    refs = refs[11:]
    if has_state:
        state_ref = refs.pop(0)
    if with_kv:
        kvn_ref, wkv_ref, rope_ref = refs[:3]
        o_ref, conv_ref, kv_ref, gbuf, act = refs[3:]
    else:
        fn_ref = refs[0]
        o_ref, conv_ref, gbuf, act = refs[1:]

    if has_state:
        gbuf[0:prefix, :] = state_ref[...]
    else:
        @pl.when(pl.program_id(0) % tiles_per_seq == 0)
        def _():
            gbuf[0:prefix, :] = jnp.zeros((prefix, D_FF), F32)

    x = x_ref[...]
    h = _rms(x, nf_ref[...]).astype(BF16)
    for n in range(FF_BLOCKS):
        sl = slice(n * MXU_COLS, (n + 1) * MXU_COLS)
        g = _dot(h, wg_ref[:, sl])
        gbuf[prefix:prefix + tm, sl] = g
        g1 = gbuf[prefix - shift:prefix - shift + tm, sl]
        g2 = gbuf[prefix - 2 * shift:prefix - 2 * shift + tm, sl]
        conv = cb_ref[:, sl] + g2 * cw_ref[0:1, sl]
        conv = conv + g1 * cw_ref[1:2, sl]
        conv = conv + g * cw_ref[2:3, sl]
        up = _dot(h, wu_ref[:, sl])
        act[:, sl] = (_gelu(conv) * up).astype(BF16)

    conv_ref[...] = gbuf[tm:tm + prefix, :]
    if not has_state:
        gbuf[0:prefix, :] = gbuf[tm:tm + prefix, :]

    x = x + _dot(act[...], wd_ref[...])
    gate = jax.nn.sigmoid(_dot((x * pn_ref[...]).astype(BF16), pg_ref[...]) * _rms_factor(x))
    x = x + _dot(p_ref[...].astype(BF16), pp_ref[...]) * gate

    if with_kv:
        o_ref[...] = x
        kv = _dot((x * kvn_ref[...]).astype(BF16), wkv_ref[...]) * _rms_factor(x)
        kv_ref[:, 0:KV_DIM] = _rope(kv[:, 0:KV_DIM], rope_ref[...])
        kv_ref[:, KV_DIM:2 * KV_DIM] = kv[:, KV_DIM:2 * KV_DIM]
    else:
        o_ref[...] = _rms(x, fn_ref[...])


def _ffn(x, p, w, layer, *, tm, seq=None, p_offset=0, time_steps=None, state=None, kv=None, final_norm=None):
    rows = x.shape[0]
    has_state = state is not None
    with_kv = kv is not None
    time_major = time_steps is not None
    if time_major:
        shift = rows // time_steps
        prefix, tiles_per_seq = (CONV_W - 1) * shift, 1
        conv_rows, conv_spec = prefix, pl.BlockSpec((prefix, D_FF), lambda i: (0, 0))
    else:
        shift, prefix = 1, SUBLANES
        tiles_per_seq = seq // tm
        conv_rows = (rows // seq) * prefix
        conv_spec = pl.BlockSpec((prefix, D_FF), lambda i: (i // tiles_per_seq, 0))

    args = [x, p, w["nf"], w["wg"], w["wu"], w["cw"], w["cb"], w["wd"], w["pn"], w["pg"], w["pp"]]
    in_specs = [
        _row_spec(tm, D_MODEL), _row_spec(tm, PLE_DIM, p_offset), _layer_spec((1, D_MODEL), layer),
        _layer_spec((D_MODEL, D_FF), layer), _layer_spec((D_MODEL, D_FF), layer),
        _layer_spec((CONV_W, D_FF), layer), _layer_spec((1, D_FF), layer), _layer_spec((D_FF, D_MODEL), layer),
        _layer_spec((1, D_MODEL), layer), _layer_spec((D_MODEL, D_MODEL), layer),
        _layer_spec((PLE_DIM, D_MODEL), layer),
    ]
    scratch = [pltpu.VMEM((prefix + tm, D_FF), F32), pltpu.VMEM((tm, D_FF), BF16)]
    if has_state:
        args.append(state)
        in_specs.append(_const_spec((prefix, D_FF)))
    out_specs = [_row_spec(tm, D_MODEL), conv_spec]
    out_shape = [jax.ShapeDtypeStruct((rows, D_MODEL), F32), jax.ShapeDtypeStruct((conv_rows, D_FF), F32)]
    if with_kv:
        kvn, wkv, rope = kv
        args += [kvn, wkv, rope]
        rope_tiles = rope.shape[0] // tm
        in_specs += [_const_spec((1, D_MODEL)), _const_spec((D_MODEL, 2 * KV_DIM)),
                     pl.BlockSpec((tm, 2 * LANES), lambda i: (i % rope_tiles, 0))]
        out_specs.append(_row_spec(tm, 2 * KV_DIM))
        out_shape.append(jax.ShapeDtypeStruct((rows, 2 * KV_DIM), F32))
    else:
        args.append(final_norm)
        in_specs.append(_const_spec((1, D_MODEL)))

    return pl.pallas_call(
        functools.partial(_ffn_kernel, tm=tm, shift=shift, prefix=prefix, tiles_per_seq=tiles_per_seq,
                          has_state=has_state, with_kv=with_kv),
        grid=(rows // tm,),
        in_specs=in_specs,
        out_specs=out_specs,
        out_shape=out_shape,
        scratch_shapes=scratch,
        compiler_params=_params(),
        name=("ffn_kv" if with_kv else "ffn_final") + ("_sample" if time_major else "_prompt"),
    )(*args)


def _q_proj(x_ref, nm_ref, wq_ref, rope_ref, q_scr):
    h = _rms(x_ref[...], nm_ref[...]).astype(BF16)
    table = rope_ref[...]
    for c in range(D_MODEL // MXU_COLS):
        qc = _dot(h, wq_ref[:, c * MXU_COLS:(c + 1) * MXU_COLS])
        for s in range(MXU_COLS // LANES):
            col = c * MXU_COLS + s * LANES
            qs = _rope(qc[:, s * LANES:(s + 1) * LANES], table) * (HEAD_DIM ** -0.5 * LOG2_E)
            q_scr[:, col:col + LANES] = qs.astype(q_scr.dtype)


def _head_frames(kv):
    lane = lax.broadcasted_iota(jnp.int32, kv.shape, 1)
    lo = lane < HEAD_DIM
    sw = pltpu.roll(kv, HEAD_DIM, 1)
    zero = jnp.zeros_like(kv)
    return [jnp.where(lo, kv, zero), jnp.where(lo, zero, sw), jnp.where(lo, sw, zero), jnp.where(lo, zero, kv)]


def _softmax_parts(s, mask, sink):
    s = jnp.where(mask, s, NEG_INF)
    sink = sink * LOG2_E
    m = jnp.maximum(jnp.max(s, axis=-1, keepdims=True), sink)
    e = jnp.exp2(s - m)
    den = jnp.sum(e, axis=-1, keepdims=True) + jnp.exp2(sink - m)
    return e, 1.0 / den


def _attend(qstack, k_lo, k_hi, v_lo, v_hi, mask, sinks_ref, first_slab, rows):
    weights, scales = [], []
    for par, k_fr in enumerate((k_lo, k_hi)):
        sc = _dot_nt(qstack, k_fr)
        w_par, r_par = [], []
        for s in range(SLABS_PER_KV):
            head = (first_slab + s) * 2 + par
            e, r = _softmax_parts(sc[s * rows:(s + 1) * rows], mask, sinks_ref[head])
            w_par.append(e.astype(BF16))
            r_par.append(r)
        weights.append(w_par)
        scales.append(r_par)
    vcat = jnp.concatenate([v_lo, v_hi], axis=0)
    wcat = jnp.concatenate(
        [jnp.concatenate([weights[0][s], weights[1][s]], axis=1) for s in range(SLABS_PER_KV)], axis=0)
    o = _dot(wcat, vcat)
    lane = lax.broadcasted_iota(jnp.int32, (rows, LANES), 1)
    return [o[s * rows:(s + 1) * rows] * jnp.where(lane < HEAD_DIM, scales[0][s], scales[1][s])
            for s in range(SLABS_PER_KV)]


def _attn_prompt_kernel(x_ref, nm_ref, wq_ref, rope_ref, kvc_ref, kvp_ref, sinks_ref, wo_ref,
                        o_ref, q_scr, kx, vx, a_scr, *, tm, tiles_per_seq):
    first_tile = pl.program_id(0) % tiles_per_seq == 0
    _q_proj(x_ref, nm_ref, wq_ref, rope_ref, q_scr)

    kv = jnp.concatenate([kvp_ref[...], kvc_ref[...]], axis=0)
    for f, fr in enumerate(_head_frames(kv[:, 0:KV_DIM])):
        kx[f] = fr.astype(BF16)
    for f, fr in enumerate(_head_frames(kv[:, KV_DIM:2 * KV_DIM])):
        vx[f] = fr.astype(BF16)

    qi = lax.broadcasted_iota(jnp.int32, (WINDOW, 2 * WINDOW), 0)
    sj = lax.broadcasted_iota(jnp.int32, (WINDOW, 2 * WINDOW), 1)
    band = (sj > qi) & (sj <= qi + WINDOW)

    def q_block(qb, carry):
        r0 = pl.multiple_of(qb * WINDOW, WINDOW)
        lo_lim = jnp.where(jnp.logical_and(first_tile, qb == 0), WINDOW, 0)
        mask = band & (sj >= lo_lim)
        for j in range(N_KV_HEADS):
            first_slab = SLABS_PER_KV * j
            qstack = jnp.concatenate(
                [q_scr[pl.ds(r0, WINDOW), (first_slab + s) * LANES:(first_slab + s + 1) * LANES]
                 for s in range(SLABS_PER_KV)], axis=0)
            keys = pl.ds(r0, 2 * WINDOW)
            outs = _attend(qstack, kx[2 * j, keys, :], kx[2 * j + 1, keys, :], vx[2 * j, keys, :],
                           vx[2 * j + 1, keys, :], mask, sinks_ref, first_slab, WINDOW)
            for s, o in enumerate(outs):
                col = (first_slab + s) * LANES
                a_scr[pl.ds(r0, WINDOW), col:col + LANES] = o.astype(BF16)
        return carry

    lax.fori_loop(0, tm // WINDOW, q_block, 0, unroll=Q_BLOCK_UNROLL)
    o_ref[...] = x_ref[...] + _dot(a_scr[...], wo_ref[...])


def _attn_prompt(x, nm, wq, rope, kv, sinks, wo, *, tm, seq):
    rows = x.shape[0]
    tiles_per_seq = seq // tm
    blocks_per_tile = tm // WINDOW
    cur = pl.BlockSpec((tm, 2 * KV_DIM), lambda i: (i, 0))
    prev = pl.BlockSpec((WINDOW, 2 * KV_DIM), lambda i: (jnp.maximum(i * blocks_per_tile - 1, 0), 0))
    return pl.pallas_call(
        functools.partial(_attn_prompt_kernel, tm=tm, tiles_per_seq=tiles_per_seq),
        grid=(rows // tm,),
        in_specs=[
            _row_spec(tm, D_MODEL), _layer_spec((1, D_MODEL), 1), _layer_spec((D_MODEL, D_MODEL), 0),
            pl.BlockSpec((tm, 2 * LANES), lambda i: (i % tiles_per_seq, 0)),
            cur, prev, _SMEM_SPEC, _layer_spec((D_MODEL, D_MODEL), 0),
        ],
        out_specs=_row_spec(tm, D_MODEL),
        out_shape=jax.ShapeDtypeStruct((rows, D_MODEL), F32),
        scratch_shapes=[
            pltpu.VMEM((tm, D_MODEL), BF16),
            pltpu.VMEM((4, WINDOW + tm, LANES), BF16),
            pltpu.VMEM((4, WINDOW + tm, LANES), BF16),
            pltpu.VMEM((tm, D_MODEL), BF16),
        ],
        compiler_params=_params(),
        name="attn_prompt",
    )(x, nm, wq, rope, kv, kv, sinks, wo)


def _attn_sample_kernel(x_ref, nm_ref, wq_ref, rope_ref, ck_ref, cv_ref, kvn_ref, sinks_ref, wo_ref,
                        o_ref, q_scr, a_scr, *, n_seq, n_new):
    _q_proj(x_ref, nm_ref, wq_ref, rope_ref, q_scr)

    rows = SEQ_GROUP * PAD_T
    n_cached = SEQ_GROUP * WINDOW
    frame = n_cached + LANES
    r = lax.broadcasted_iota(jnp.int32, (rows, frame), 0)
    c = lax.broadcasted_iota(jnp.int32, (rows, frame), 1)
    t_bits, w_bits = PAD_T.bit_length() - 1, WINDOW.bit_length() - 1
    q_seq, q_t = r >> t_bits, r & (PAD_T - 1)
    fresh = c - n_cached
    fresh_t = fresh & (PAD_T - 1)
    seen_cached = ((c >> w_bits) == q_seq) & ((c & (WINDOW - 1)) > q_t)
    seen_fresh = ((fresh >> t_bits) == q_seq) & (fresh_t <= q_t) & (fresh_t < n_new)
    mask = seen_cached | seen_fresh
    fresh_pad = jnp.zeros((LANES - rows, KV_DIM), F32)

    def seq_group(g, carry):
        r0 = pl.multiple_of(g * rows, rows)
        seqs = pl.ds(g * SEQ_GROUP, SEQ_GROUP)
        kv_new = kvn_ref[pl.ds(r0, rows), :]
        k_fr = [f.astype(BF16) for f in _head_frames(jnp.concatenate(
            [ck_ref[seqs].reshape(n_cached, KV_DIM), kv_new[:, 0:KV_DIM], fresh_pad], axis=0))]
        v_fr = [f.astype(BF16) for f in _head_frames(jnp.concatenate(
            [cv_ref[seqs].reshape(n_cached, KV_DIM), kv_new[:, KV_DIM:2 * KV_DIM], fresh_pad], axis=0))]
        for j in range(N_KV_HEADS):
            first_slab = SLABS_PER_KV * j
            qstack = jnp.concatenate(
                [q_scr[pl.ds(r0, rows), (first_slab + s) * LANES:(first_slab + s + 1) * LANES]
                 for s in range(SLABS_PER_KV)], axis=0)
            outs = _attend(qstack, k_fr[2 * j], k_fr[2 * j + 1], v_fr[2 * j], v_fr[2 * j + 1], mask, sinks_ref,
                           first_slab, rows)
            for s, o in enumerate(outs):
                col = (first_slab + s) * LANES
                a_scr[pl.ds(r0, rows), col:col + LANES] = o.astype(BF16)
        return carry

    lax.fori_loop(0, n_seq // SEQ_GROUP, seq_group, 0)
    o_ref[...] = x_ref[...] + _dot(a_scr[...], wo_ref[...])


def _attn_sample(x, nm, wq, rope, ck, cv, kvn, sinks, wo, *, n_seq, n_new):
    rows = x.shape[0]
    full = lambda shape: pl.BlockSpec(shape, lambda i: (0,) * len(shape))
    return pl.pallas_call(
        functools.partial(_attn_sample_kernel, n_seq=n_seq, n_new=n_new),
        grid=(1,),
        in_specs=[
            full((rows, D_MODEL)), _layer_spec((1, D_MODEL), 1), _layer_spec((D_MODEL, D_MODEL), 0),
            full((rows, 2 * LANES)), full(ck.shape), full(cv.shape), full(kvn.shape),
            _SMEM_SPEC, _layer_spec((D_MODEL, D_MODEL), 0),
        ],
        out_specs=full((rows, D_MODEL)),
        out_shape=jax.ShapeDtypeStruct((rows, D_MODEL), F32),
        scratch_shapes=[pltpu.VMEM((rows, D_MODEL), BF16), pltpu.VMEM((rows, D_MODEL), BF16)],
        compiler_params=_params(),
        name="attn_sample",
    )(x, nm, wq, rope, ck, cv, kvn, sinks, wo)


def _rope_table(pos):
    dim = np.arange(LANES) % HEAD_DIM
    inv_freq = ROPE_THETA ** (-jnp.arange(ROT_HALF, dtype=F32) / ROT_HALF)
    ang = pos.astype(F32)[:, None] * inv_freq[dim % ROT_HALF][None, :]
    in_rot = jnp.asarray(dim < ROT_DIM)[None, :]
    first = jnp.asarray(dim < ROT_HALF)[None, :]
    sin = jnp.sin(ang)
    c = jnp.where(in_rot, jnp.cos(ang), 1.0)
    s = jnp.where(first, -sin, jnp.where(in_rot, sin, 0.0))
    return jnp.concatenate([c, s], axis=1)


def kernel(x_prompt, x_sample, state_ffn_conv, cache_k_win, cache_v_win, p_prompt, p_sample, norm_mix, gm_w_in,
           gm_v_norm, gm_w_s, gm_b_s, gm_w_out, kv_norm, w_kv, w_q, attn_sinks, w_o, norm_ffn, ffn_w_gate,
           ffn_w_up, ffn_conv_w, ffn_conv_b, ffn_w_down, ple_norm, ple_w_gate, ple_w_proj, final_norm):
    batch, seq, _ = x_prompt.shape
    dec_batch, dec_seq, _ = x_sample.shape
    depth = norm_mix.shape[0]
    assert depth == 2 and gm_w_in.shape[0] == 1 and w_q.shape[0] == 1
    tm, tm_ffn = 1024, 512
    assert seq % tm == 0 and seq % tm_ffn == 0 and dec_seq <= PAD_T and dec_batch % SEQ_GROUP == 0

    rows3 = lambda a: a.reshape(a.shape[0], 1, a.shape[1])
    ffn_w = dict(
        nf=rows3(norm_ffn), wg=ffn_w_gate.astype(BF16), wu=ffn_w_up.astype(BF16), cw=ffn_conv_w,
        cb=rows3(ffn_conv_b), wd=ffn_w_down.astype(BF16), pn=rows3(ple_norm), pg=ple_w_gate.astype(BF16),
        pp=ple_w_proj.astype(BF16))
    w_in = gm_w_in.astype(BF16)
    w_out = gm_w_out.astype(BF16)
    wq = w_q.astype(BF16)
    wo = w_o.astype(BF16)
    wkv = w_kv.astype(BF16)
    kvn = kv_norm.reshape(1, D_MODEL)
    nm = rows3(norm_mix)
    vn = rows3(gm_v_norm)
    fn = final_norm.reshape(1, D_MODEL)
    sinks = attn_sinks[0]

    n_rows_p = batch * seq
    rope_p = _rope_table(jnp.arange(seq, dtype=jnp.int32))
    bias_p = jnp.repeat(gm_b_s[0].T, GM_GROUP_DIM, axis=1)
    xp = x_prompt.reshape(n_rows_p, D_MODEL)
    p_p = p_prompt.reshape(depth * n_rows_p, PLE_DIM)
    (xp,) = _gmlp(xp, nm, w_in, vn, w_out, tm=tm, ws=gm_w_s, bias=bias_p)
    xp, conv_p0, kv_p = _ffn(xp, p_p, ffn_w, 0, tm=tm_ffn, seq=seq, kv=(kvn, wkv, rope_p))
    xp = _attn_prompt(xp, nm, wq, rope_p, kv_p, sinks, wo, tm=tm, seq=seq)
    y_p, conv_p1 = _ffn(xp, p_p, ffn_w, 1, tm=tm_ffn, seq=seq, p_offset=n_rows_p // tm_ffn, final_norm=fn)

    y_prompt = y_p.reshape(batch, seq, D_MODEL)
    new_conv_prompt = jnp.stack(
        [c.reshape(batch, SUBLANES, D_FF)[:, SUBLANES - (CONV_W - 1):] for c in (conv_p0, conv_p1)], axis=0)
    kv_tail = kv_p.reshape(batch, seq, 2 * KV_DIM)[:, seq - WINDOW:]
    new_k_prompt = kv_tail[:, :, :KV_DIM].reshape(batch, WINDOW, N_KV_HEADS, HEAD_DIM)
    new_v_prompt = kv_tail[:, :, KV_DIM:].reshape(batch, WINDOW, N_KV_HEADS, HEAD_DIM)

    n_rows = dec_batch * dec_seq
    to_tm = lambda a: jnp.swapaxes(a, 0, 1).reshape(a.shape[0] * a.shape[1], a.shape[2])
    from_tm = lambda a, t: jnp.swapaxes(a.reshape(t, dec_batch, a.shape[-1]), 0, 1)
    pos_s = PAST_LEN + jnp.arange(dec_seq, dtype=jnp.int32)
    rope_s = _rope_table(jnp.repeat(pos_s, dec_batch))
    xs = to_tm(x_sample)
    ws_flat = gm_w_s[0][:, :dec_seq, :dec_seq].reshape(-1)
    bs_flat = gm_b_s[0][:, :dec_seq].reshape(-1)
    xs, gm_v = _gmlp(xs, nm, w_in, vn, w_out, tm=n_rows, ws_flat=ws_flat, bs_flat=bs_flat, nt=dec_seq)
    xs, conv_s0, kv_s = _ffn(xs, to_tm(p_sample[0]), ffn_w, 0, tm=n_rows, time_steps=dec_seq,
                             state=to_tm(state_ffn_conv[0]), kv=(kvn, wkv, rope_s))

    pad_bm = lambda a: jnp.pad(from_tm(a, dec_seq), ((0, 0), (0, PAD_T - dec_seq), (0, 0))).reshape(
        dec_batch * PAD_T, a.shape[-1])
    rope_a = _rope_table(jnp.tile(PAST_LEN + jnp.arange(PAD_T, dtype=jnp.int32), dec_batch))
    xa = _attn_sample(pad_bm(xs), nm, wq, rope_a,
                      cache_k_win.reshape(dec_batch, WINDOW, KV_DIM), cache_v_win.reshape(dec_batch, WINDOW, KV_DIM),
                      pad_bm(kv_s), sinks, wo, n_seq=dec_batch, n_new=dec_seq)
    xs = to_tm(xa.reshape(dec_batch, PAD_T, D_MODEL)[:, :dec_seq])
    y_s, conv_s1 = _ffn(xs, to_tm(p_sample[1]), ffn_w, 1, tm=n_rows, time_steps=dec_seq,
                        state=to_tm(state_ffn_conv[1]), final_norm=fn)

    y_sample = from_tm(y_s, dec_seq)
    new_gm_v_sample = from_tm(gm_v, dec_seq)[None]
    new_conv_sample = jnp.stack([from_tm(c, CONV_W - 1) for c in (conv_s0, conv_s1)], axis=0)
    kv_bm = from_tm(kv_s, dec_seq)
    new_k_sample = kv_bm[:, :, :KV_DIM].reshape(dec_batch, dec_seq, N_KV_HEADS, HEAD_DIM)
    new_v_sample = kv_bm[:, :, KV_DIM:].reshape(dec_batch, dec_seq, N_KV_HEADS, HEAD_DIM)

    return (y_prompt, y_sample, new_gm_v_sample, new_conv_prompt, new_conv_sample,
            new_k_prompt, new_v_prompt, new_k_sample, new_v_sample)
```

```python
import functools

import numpy as np
import jax
import jax.numpy as jnp
from jax import lax
from jax.experimental import pallas as pl
from jax.experimental.pallas import tpu as pltpu

D_MODEL = 1024
GM_HALF = 2 * D_MODEL
GM_GROUPS = 8
GM_GROUP_DIM = GM_HALF // GM_GROUPS
CHUNK = 128
HEAD_DIM = 64
N_HEADS = D_MODEL // HEAD_DIM
N_KV_HEADS = N_HEADS // 8
KV_DIM = N_KV_HEADS * HEAD_DIM
WINDOW = 128
ROT_DIM = HEAD_DIM // 4
ROT_HALF = ROT_DIM // 2
ROPE_THETA = 500000.0
D_FF = 2816
CONV_W = 3
PLE_DIM = 256
EPS = 1e-6
NEG_INF = -1e30
LOG2_E = 1.4426950408889634
PAST_LEN = 16384

LANES = 128
SUBLANES = 8
MXU_COLS = 256
VMEM_LIMIT_BYTES = 56 * 1024 * 1024
SLABS = D_MODEL // LANES
SLABS_PER_KV = SLABS // N_KV_HEADS
FF_BLOCKS = D_FF // MXU_COLS
SEQ_GROUP = 8
Q_BLOCK_UNROLL = 4

BF16 = jnp.bfloat16
F32 = jnp.float32


def _rms_factor(x):
    ms = jnp.sum(x * x, axis=-1, keepdims=True) * (1.0 / x.shape[-1])
    return lax.rsqrt(ms + EPS)


def _rms(x, w):
    return x * _rms_factor(x) * w


def _dot(a, b):
    return jnp.dot(a, b, preferred_element_type=F32)


def _dot_nt(a, b):
    return lax.dot_general(a, b, (((1,), (1,)), ((), ())), preferred_element_type=F32)


def _gelu(x):
    return jax.nn.gelu(x, approximate=True)


def _rope(x, table):
    c = table[:, 0:LANES]
    s = table[:, LANES:2 * LANES]
    lane = lax.broadcasted_iota(jnp.int32, x.shape, 1)
    partner = jnp.where((lane & ROT_HALF) == 0, pltpu.roll(x, LANES - ROT_HALF, 1), pltpu.roll(x, ROT_HALF, 1))
    return x * c + partner * s


def _const_spec(shape):
    zeros = (0,) * len(shape)
    return pl.BlockSpec(shape, lambda i: zeros, pipeline_mode=pl.Buffered(1))


def _layer_spec(shape, layer):
    idx = (layer,) + (0,) * len(shape)
    return pl.BlockSpec((None,) + tuple(shape), lambda i: idx, pipeline_mode=pl.Buffered(1))


def _row_spec(tm, width, offset=0):
    return pl.BlockSpec((tm, width), lambda i: (i + offset, 0))


_SMEM_SPEC = pl.BlockSpec(memory_space=pltpu.SMEM)


def _params():
    return pltpu.CompilerParams(dimension_semantics=("arbitrary",), vmem_limit_bytes=VMEM_LIMIT_BYTES)


def _gmlp_kernel(*refs, tm, time_major, nt=None):
    if time_major:
        x_ref, nw_ref, win_ref, vn_ref, ws_ref, bs_ref, wout_ref, o_ref, v_ref, v_scr, out_scr = refs
    else:
        x_ref, nw_ref, win_ref, vn_ref, ws_ref, bias_ref, wout_ref, o_ref, v_scr, out_scr = refs

    x = x_ref[...]
    h = _rms(x, nw_ref[...]).astype(BF16)

    ssq = jnp.zeros((tm, 1), F32)
    for g in range(GM_GROUPS):
        sl = slice(g * GM_GROUP_DIM, (g + 1) * GM_GROUP_DIM)
        vg = _gelu(_dot(h, win_ref[:, GM_HALF + g * GM_GROUP_DIM:GM_HALF + (g + 1) * GM_GROUP_DIM]))
        v_scr[:, sl] = vg
        ssq = ssq + jnp.sum(vg * vg, axis=-1, keepdims=True)
    rinv = lax.rsqrt(ssq * (1.0 / GM_HALF) + EPS)

    if not time_major:
        row = lax.broadcasted_iota(jnp.int32, (CHUNK, CHUNK), 0)
        col = lax.broadcasted_iota(jnp.int32, (CHUNK, CHUNK), 1)
        causal = col <= row

    for g in range(GM_GROUPS):
        sl = slice(g * GM_GROUP_DIM, (g + 1) * GM_GROUP_DIM)
        vg = v_scr[:, sl] * rinv * vn_ref[:, sl]
        ug = _gelu(_dot(h, win_ref[:, sl]))
        if time_major:
            v_ref[:, sl] = vg
            nb = tm // nt
            for t in range(nt):
                m = ws_ref[(g * nt + t) * nt] * vg[0:nb]
                for s in range(1, t + 1):
                    m = m + ws_ref[(g * nt + t) * nt + s] * vg[s * nb:(s + 1) * nb]
                m = m + bs_ref[g * nt + t]
                out_scr[t * nb:(t + 1) * nb, sl] = (ug[t * nb:(t + 1) * nb] * m).astype(BF16)
        else:
            vb = vg.astype(BF16)
            wm = jnp.where(causal, ws_ref[g], 0.0).astype(BF16)
            for c in range(tm // CHUNK):
                rs = slice(c * CHUNK, (c + 1) * CHUNK)
                m = _dot(wm, vb[rs]) + bias_ref[:, sl]
                out_scr[rs, sl] = (ug[rs] * m).astype(BF16)

    o_ref[...] = x + _dot(out_scr[...], wout_ref[...])


def _gmlp(x, nw, w_in, vn, w_out, *, tm, ws=None, bias=None, ws_flat=None, bs_flat=None, nt=None):
    rows = x.shape[0]
    time_major = ws is None
    in_specs = [_row_spec(tm, D_MODEL), _layer_spec((1, D_MODEL), 0), _layer_spec((D_MODEL, 2 * GM_HALF), 0),
                _layer_spec((1, GM_HALF), 0)]
    out_specs = [_row_spec(tm, D_MODEL)]
    out_shape = [jax.ShapeDtypeStruct((rows, D_MODEL), F32)]
    if time_major:
        args = [x, nw, w_in, vn, ws_flat, bs_flat, w_out]
        in_specs += [_SMEM_SPEC, _SMEM_SPEC]
        out_specs.append(_row_spec(tm, GM_HALF))
        out_shape.append(jax.ShapeDtypeStruct((rows, GM_HALF), F32))
    else:
        args = [x, nw, w_in, vn, ws, bias, w_out]
        in_specs += [_layer_spec((GM_GROUPS, CHUNK, CHUNK), 0), _const_spec((CHUNK, GM_HALF))]
    in_specs.append(_layer_spec((GM_HALF, D_MODEL), 0))
    return pl.pallas_call(
        functools.partial(_gmlp_kernel, tm=tm, time_major=time_major, nt=nt),
        grid=(rows // tm,),
        in_specs=in_specs,
        out_specs=out_specs,
        out_shape=out_shape,
        scratch_shapes=[pltpu.VMEM((tm, GM_HALF), F32), pltpu.VMEM((tm, GM_HALF), BF16)],
        compiler_params=_params(),
        name="gmlp_sample" if time_major else "gmlp_prompt",
    )(*args)


def _ffn_kernel(*refs, tm, shift, prefix, tiles_per_seq, has_state, with_kv):
    refs = list(refs)
    x_ref, p_ref, nf_ref, wg_ref, wu_ref, cw_ref, cb_ref, wd_ref, pn_ref, pg_ref, pp_ref = refs[:11]
    refs = refs[11:]
    if has_state:
        state_ref = refs.pop(0)
    if with_kv:
        kvn_ref, wkv_ref, rope_ref = refs[:3]
        o_ref, conv_ref, kv_ref, gbuf, act = refs[3:]
    else:
        fn_ref = refs[0]
        o_ref, conv_ref, gbuf, act = refs[1:]

    if has_state:
        gbuf[0:prefix, :] = state_ref[...]
    else:
        @pl.when(pl.program_id(0) % tiles_per_seq == 0)
        def _():
            gbuf[0:prefix, :] = jnp.zeros((prefix, D_FF), F32)

    x = x_ref[...]
    h = _rms(x, nf_ref[...]).astype(BF16)
    for n in range(FF_BLOCKS):
        sl = slice(n * MXU_COLS, (n + 1) * MXU_COLS)
        g = _dot(h, wg_ref[:, sl])
        gbuf[prefix:prefix + tm, sl] = g
        g1 = gbuf[prefix - shift:prefix - shift + tm, sl]
        g2 = gbuf[prefix - 2 * shift:prefix - 2 * shift + tm, sl]
        conv = cb_ref[:, sl] + g2 * cw_ref[0:1, sl]
        conv = conv + g1 * cw_ref[1:2, sl]
        conv = conv + g * cw_ref[2:3, sl]
        up = _dot(h, wu_ref[:, sl])
        act[:, sl] = (_gelu(conv) * up).astype(BF16)

    conv_ref[...] = gbuf[tm:tm + prefix, :]
    if not has_state:
        gbuf[0:prefix, :] = gbuf[tm:tm + prefix, :]

    x = x + _dot(act[...], wd_ref[...])
    gate = jax.nn.sigmoid(_dot((x * pn_ref[...]).astype(BF16), pg_ref[...]) * _rms_factor(x))
    x = x + _dot(p_ref[...].astype(BF16), pp_ref[...]) * gate

    if with_kv:
        o_ref[...] = x
        kv = _dot((x * kvn_ref[...]).astype(BF16), wkv_ref[...]) * _rms_factor(x)
        kv_ref[:, 0:KV_DIM] = _rope(kv[:, 0:KV_DIM], rope_ref[...])
        kv_ref[:, KV_DIM:2 * KV_DIM] = kv[:, KV_DIM:2 * KV_DIM]
    else:
        o_ref[...] = _rms(x, fn_ref[...])


def _ffn(x, p, w, layer, *, tm, seq=None, p_offset=0, time_steps=None, state=None, kv=None, final_norm=None):
    rows = x.shape[0]
    has_state = state is not None
    with_kv = kv is not None
    time_major = time_steps is not None
    if time_major:
        shift = rows // time_steps
        prefix, tiles_per_seq = (CONV_W - 1) * shift, 1
        conv_rows, conv_spec = prefix, pl.BlockSpec((prefix, D_FF), lambda i: (0, 0))
    else:
        shift, prefix = 1, SUBLANES
        tiles_per_seq = seq // tm
        conv_rows = (rows // seq) * prefix
        conv_spec = pl.BlockSpec((prefix, D_FF), lambda i: (i // tiles_per_seq, 0))

    args = [x, p, w["nf"], w["wg"], w["wu"], w["cw"], w["cb"], w["wd"], w["pn"], w["pg"], w["pp"]]
    in_specs = [
        _row_spec(tm, D_MODEL), _row_spec(tm, PLE_DIM, p_offset), _layer_spec((1, D_MODEL), layer),
        _layer_spec((D_MODEL, D_FF), layer), _layer_spec((D_MODEL, D_FF), layer),
        _layer_spec((CONV_W, D_FF), layer), _layer_spec((1, D_FF), layer), _layer_spec((D_FF, D_MODEL), layer),
        _layer_spec((1, D_MODEL), layer), _layer_spec((D_MODEL, D_MODEL), layer),
        _layer_spec((PLE_DIM, D_MODEL), layer),
    ]
    scratch = [pltpu.VMEM((prefix + tm, D_FF), F32), pltpu.VMEM((tm, D_FF), BF16)]
    if has_state:
        args.append(state)
        in_specs.append(pl.BlockSpec((prefix, D_FF), lambda i: (layer, 0), pipeline_mode=pl.Buffered(1)))
    out_specs = [_row_spec(tm, D_MODEL), conv_spec]
    out_shape = [jax.ShapeDtypeStruct((rows, D_MODEL), F32), jax.ShapeDtypeStruct((conv_rows, D_FF), F32)]
    if with_kv:
        kvn, wkv, rope = kv
        args += [kvn, wkv, rope]
        rope_tiles = rope.shape[0] // tm
        in_specs += [_const_spec((1, D_MODEL)), _const_spec((D_MODEL, 2 * KV_DIM)),
                     pl.BlockSpec((tm, 2 * LANES), lambda i: (i % rope_tiles, 0))]
        out_specs.append(_row_spec(tm, 2 * KV_DIM))
        out_shape.append(jax.ShapeDtypeStruct((rows, 2 * KV_DIM), F32))
    else:
        args.append(final_norm)
        in_specs.append(_const_spec((1, D_MODEL)))

    return pl.pallas_call(
        functools.partial(_ffn_kernel, tm=tm, shift=shift, prefix=prefix, tiles_per_seq=tiles_per_seq,
                          has_state=has_state, with_kv=with_kv),
        grid=(rows // tm,),
        in_specs=in_specs,
        out_specs=out_specs,
        out_shape=out_shape,
        scratch_shapes=scratch,
        compiler_params=_params(),
        name=("ffn_kv" if with_kv else "ffn_final") + ("_sample" if time_major else "_prompt"),
    )(*args)


def _q_proj(x_ref, nm_ref, wq_ref, rope_ref, q_scr):
    h = _rms(x_ref[...], nm_ref[...]).astype(BF16)
    table = rope_ref[...]
    for c in range(D_MODEL // MXU_COLS):
        qc = _dot(h, wq_ref[:, c * MXU_COLS:(c + 1) * MXU_COLS])
        for s in range(MXU_COLS // LANES):
            col = c * MXU_COLS + s * LANES
            qs = _rope(qc[:, s * LANES:(s + 1) * LANES], table) * (HEAD_DIM ** -0.5 * LOG2_E)
            q_scr[:, col:col + LANES] = qs.astype(q_scr.dtype)


def _head_frames(kv):
    lane = lax.broadcasted_iota(jnp.int32, kv.shape, 1)
    lo = lane < HEAD_DIM
    sw = pltpu.roll(kv, HEAD_DIM, 1)
    zero = jnp.zeros_like(kv)
    return [jnp.where(lo, kv, zero), jnp.where(lo, zero, sw), jnp.where(lo, sw, zero), jnp.where(lo, zero, kv)]


def _softmax_parts(s, mask, sink):
    s = jnp.where(mask, s, NEG_INF)
    sink = sink * LOG2_E
    m = jnp.maximum(jnp.max(s, axis=-1, keepdims=True), sink)
    e = jnp.exp2(s - m)
    den = jnp.sum(e, axis=-1, keepdims=True) + jnp.exp2(sink - m)
    return e, 1.0 / den


def _attend(qstack, k_lo, k_hi, v_lo, v_hi, mask, sinks_ref, first_slab, rows):
    weights, scales = [], []
    for par, k_fr in enumerate((k_lo, k_hi)):
        sc = _dot_nt(qstack, k_fr)
        w_par, r_par = [], []
        for s in range(SLABS_PER_KV):
            head = (first_slab + s) * 2 + par
            e, r = _softmax_parts(sc[s * rows:(s + 1) * rows], mask, sinks_ref[head])
            w_par.append(e.astype(BF16))
            r_par.append(r)
        weights.append(w_par)
        scales.append(r_par)
    vcat = jnp.concatenate([v_lo, v_hi], axis=0)
    wcat = jnp.concatenate(
        [jnp.concatenate([weights[0][s], weights[1][s]], axis=1) for s in range(SLABS_PER_KV)], axis=0)
    o = _dot(wcat, vcat)
    lane = lax.broadcasted_iota(jnp.int32, (rows, LANES), 1)
    return [o[s * rows:(s + 1) * rows] * jnp.where(lane < HEAD_DIM, scales[0][s], scales[1][s])
            for s in range(SLABS_PER_KV)]


def _attn_prompt_kernel(x_ref, nm_ref, wq_ref, rope_ref, kvc_ref, kvp_ref, sinks_ref, wo_ref,
                        o_ref, q_scr, kx, vx, a_scr, *, tm, tiles_per_seq):
    first_tile = pl.program_id(0) % tiles_per_seq == 0
    _q_proj(x_ref, nm_ref, wq_ref, rope_ref, q_scr)

    kv = jnp.concatenate([kvp_ref[...], kvc_ref[...]], axis=0)
    for f, fr in enumerate(_head_frames(kv[:, 0:KV_DIM])):
        kx[f] = fr.astype(BF16)
    for f, fr in enumerate(_head_frames(kv[:, KV_DIM:2 * KV_DIM])):
        vx[f] = fr.astype(BF16)

    qi = lax.broadcasted_iota(jnp.int32, (WINDOW, 2 * WINDOW), 0)
    sj = lax.broadcasted_iota(jnp.int32, (WINDOW, 2 * WINDOW), 1)
    band = (sj > qi) & (sj <= qi + WINDOW)

    def q_block(qb, carry):
        r0 = pl.multiple_of(qb * WINDOW, WINDOW)
        lo_lim = jnp.where(jnp.logical_and(first_tile, qb == 0), WINDOW, 0)
        mask = band & (sj >= lo_lim)
        for j in range(N_KV_HEADS):
            first_slab = SLABS_PER_KV * j
            qstack = jnp.concatenate(
                [q_scr[pl.ds(r0, WINDOW), (first_slab + s) * LANES:(first_slab + s + 1) * LANES]
                 for s in range(SLABS_PER_KV)], axis=0)
            keys = pl.ds(r0, 2 * WINDOW)
            outs = _attend(qstack, kx[2 * j, keys, :], kx[2 * j + 1, keys, :], vx[2 * j, keys, :],
                           vx[2 * j + 1, keys, :], mask, sinks_ref, first_slab, WINDOW)
            for s, o in enumerate(outs):
                col = (first_slab + s) * LANES
                a_scr[pl.ds(r0, WINDOW), col:col + LANES] = o.astype(BF16)
        return carry

    lax.fori_loop(0, tm // WINDOW, q_block, 0, unroll=Q_BLOCK_UNROLL)
    o_ref[...] = x_ref[...] + _dot(a_scr[...], wo_ref[...])


def _attn_prompt(x, nm, wq, rope, kv, sinks, wo, *, tm, seq):
    rows = x.shape[0]
    tiles_per_seq = seq // tm
    blocks_per_tile = tm // WINDOW
    cur = pl.BlockSpec((tm, 2 * KV_DIM), lambda i: (i, 0))
    prev = pl.BlockSpec((WINDOW, 2 * KV_DIM), lambda i: (jnp.maximum(i * blocks_per_tile - 1, 0), 0))
    return pl.pallas_call(
        functools.partial(_attn_prompt_kernel, tm=tm, tiles_per_seq=tiles_per_seq),
        grid=(rows // tm,),
        in_specs=[
            _row_spec(tm, D_MODEL), _layer_spec((1, D_MODEL), 1), _layer_spec((D_MODEL, D_MODEL), 0),
            pl.BlockSpec((tm, 2 * LANES), lambda i: (i % tiles_per_seq, 0)),
            cur, prev, _SMEM_SPEC, _layer_spec((D_MODEL, D_MODEL), 0),
        ],
        out_specs=_row_spec(tm, D_MODEL),
        out_shape=jax.ShapeDtypeStruct((rows, D_MODEL), F32),
        scratch_shapes=[
            pltpu.VMEM((tm, D_MODEL), BF16),
            pltpu.VMEM((4, WINDOW + tm, LANES), BF16),
            pltpu.VMEM((4, WINDOW + tm, LANES), BF16),
            pltpu.VMEM((tm, D_MODEL), BF16),
        ],
        compiler_params=_params(),
        name="attn_prompt",
    )(x, nm, wq, rope, kv, kv, sinks, wo)


def _attn_sample_kernel(x_ref, nm_ref, wq_ref, rope_ref, ck_ref, cv_ref, kvn_ref, sinks_ref, wo_ref,
                        o_ref, q_scr, a_scr, *, n_seq, n_new):
    _q_proj(x_ref, nm_ref, wq_ref, rope_ref, q_scr)

    rows = SEQ_GROUP * n_new
    n_cached = SEQ_GROUP * WINDOW
    frame = n_cached + LANES
    r = lax.broadcasted_iota(jnp.int32, (rows, frame), 0)
    c = lax.broadcasted_iota(jnp.int32, (rows, frame), 1)
    t_bits, w_bits = n_new.bit_length() - 1, WINDOW.bit_length() - 1
    q_seq, q_t = r >> t_bits, r & (n_new - 1)
    fresh = c - n_cached
    seen_cached = ((c >> w_bits) == q_seq) & ((c & (WINDOW - 1)) > q_t)
    seen_fresh = ((fresh >> t_bits) == q_seq) & ((fresh & (n_new - 1)) <= q_t)
    mask = seen_cached | seen_fresh
    fresh_pad = jnp.zeros((LANES - rows, KV_DIM), F32)

    def seq_group(g, carry):
        r0 = pl.multiple_of(g * rows, rows)
        seqs = pl.ds(g * SEQ_GROUP, SEQ_GROUP)
        kv_new = kvn_ref[pl.ds(r0, rows), :]
        k_fr = [f.astype(BF16) for f in _head_frames(jnp.concatenate(
            [ck_ref[seqs].reshape(n_cached, KV_DIM), kv_new[:, 0:KV_DIM], fresh_pad], axis=0))]
        v_fr = [f.astype(BF16) for f in _head_frames(jnp.concatenate(
            [cv_ref[seqs].reshape(n_cached, KV_DIM), kv_new[:, KV_DIM:2 * KV_DIM], fresh_pad], axis=0))]
        for j in range(N_KV_HEADS):
            first_slab = SLABS_PER_KV * j
            qstack = jnp.concatenate(
                [q_scr[pl.ds(r0, rows), (first_slab + s) * LANES:(first_slab + s + 1) * LANES]
                 for s in range(SLABS_PER_KV)], axis=0)
            outs = _attend(qstack, k_fr[2 * j], k_fr[2 * j + 1], v_fr[2 * j], v_fr[2 * j + 1], mask, sinks_ref,
                           first_slab, rows)
            for s, o in enumerate(outs):
                col = (first_slab + s) * LANES
                a_scr[pl.ds(r0, rows), col:col + LANES] = o.astype(BF16)
        return carry

    lax.fori_loop(0, n_seq // SEQ_GROUP, seq_group, 0)
    o_ref[...] = x_ref[...] + _dot(a_scr[...], wo_ref[...])


def _attn_sample(x, nm, wq, rope, ck, cv, kvn, sinks, wo, *, n_seq, n_new):
    rows = x.shape[0]
    full = lambda shape: pl.BlockSpec(shape, lambda i: (0,) * len(shape))
    return pl.pallas_call(
        functools.partial(_attn_sample_kernel, n_seq=n_seq, n_new=n_new),
        grid=(1,),
        in_specs=[
            full((rows, D_MODEL)), _layer_spec((1, D_MODEL), 1), _layer_spec((D_MODEL, D_MODEL), 0),
            full((rows, 2 * LANES)), full(ck.shape), full(cv.shape), full(kvn.shape),
            _SMEM_SPEC, _layer_spec((D_MODEL, D_MODEL), 0),
        ],
        out_specs=full((rows, D_MODEL)),
        out_shape=jax.ShapeDtypeStruct((rows, D_MODEL), F32),
        scratch_shapes=[pltpu.VMEM((rows, D_MODEL), BF16), pltpu.VMEM((rows, D_MODEL), BF16)],
        compiler_params=_params(),
        name="attn_sample",
    )(x, nm, wq, rope, ck, cv, kvn, sinks, wo)


def _rope_table(pos):
    dim = np.arange(LANES) % HEAD_DIM
    inv_freq = ROPE_THETA ** (-jnp.arange(ROT_HALF, dtype=F32) / ROT_HALF)
    ang = pos.astype(F32)[:, None] * inv_freq[dim % ROT_HALF][None, :]
    in_rot = jnp.asarray(dim < ROT_DIM)[None, :]
    first = jnp.asarray(dim < ROT_HALF)[None, :]
    sin = jnp.sin(ang)
    c = jnp.where(in_rot, jnp.cos(ang), 1.0)
    s = jnp.where(first, -sin, jnp.where(in_rot, sin, 0.0))
    return jnp.concatenate([c, s], axis=1)


def kernel(x_prompt, x_sample, state_ffn_conv, cache_k_win, cache_v_win, p_prompt, p_sample, norm_mix, gm_w_in,
           gm_v_norm, gm_w_s, gm_b_s, gm_w_out, kv_norm, w_kv, w_q, attn_sinks, w_o, norm_ffn, ffn_w_gate,
           ffn_w_up, ffn_conv_w, ffn_conv_b, ffn_w_down, ple_norm, ple_w_gate, ple_w_proj, final_norm):
    batch, seq, _ = x_prompt.shape
    dec_batch, dec_seq, _ = x_sample.shape
    depth = norm_mix.shape[0]
    assert depth == 2 and gm_w_in.shape[0] == 1 and w_q.shape[0] == 1
    tm, tm_ffn = 1024, 512
    assert seq % tm == 0 and seq % tm_ffn == 0 and dec_batch % SEQ_GROUP == 0
    assert dec_seq & (dec_seq - 1) == 0 and (SEQ_GROUP * dec_seq) % (2 * SUBLANES) == 0
    assert SEQ_GROUP * dec_seq <= LANES

    rows3 = lambda a: a.reshape(a.shape[0], 1, a.shape[1])
    ffn_w = dict(
        nf=rows3(norm_ffn), wg=ffn_w_gate.astype(BF16), wu=ffn_w_up.astype(BF16), cw=ffn_conv_w,
        cb=rows3(ffn_conv_b), wd=ffn_w_down.astype(BF16), pn=rows3(ple_norm), pg=ple_w_gate.astype(BF16),
        pp=ple_w_proj.astype(BF16))
    w_in = gm_w_in.astype(BF16)
    w_out = gm_w_out.astype(BF16)
    wq = w_q.astype(BF16)
    wo = w_o.astype(BF16)
    wkv = w_kv.astype(BF16)
    kvn = kv_norm.reshape(1, D_MODEL)
    nm = rows3(norm_mix)
    vn = rows3(gm_v_norm)
    fn = final_norm.reshape(1, D_MODEL)
    sinks = attn_sinks[0]

    n_rows_p = batch * seq
    rope_p = _rope_table(jnp.arange(seq, dtype=jnp.int32))
    bias_p = jnp.repeat(gm_b_s[0].T, GM_GROUP_DIM, axis=1)
    xp = x_prompt.reshape(n_rows_p, D_MODEL)
    p_p = p_prompt.reshape(depth * n_rows_p, PLE_DIM)
    (xp,) = _gmlp(xp, nm, w_in, vn, w_out, tm=tm, ws=gm_w_s, bias=bias_p)
    xp, conv_p0, kv_p = _ffn(xp, p_p, ffn_w, 0, tm=tm_ffn, seq=seq, kv=(kvn, wkv, rope_p))
    xp = _attn_prompt(xp, nm, wq, rope_p, kv_p, sinks, wo, tm=tm, seq=seq)
    y_p, conv_p1 = _ffn(xp, p_p, ffn_w, 1, tm=tm_ffn, seq=seq, p_offset=n_rows_p // tm_ffn, final_norm=fn)

    y_prompt = y_p.reshape(batch, seq, D_MODEL)
    new_conv_prompt = jnp.stack(
        [c.reshape(batch, SUBLANES, D_FF)[:, SUBLANES - (CONV_W - 1):] for c in (conv_p0, conv_p1)], axis=0)
    kv_tail = kv_p.reshape(batch, seq, 2 * KV_DIM)[:, seq - WINDOW:]
    new_k_prompt = kv_tail[:, :, :KV_DIM].reshape(batch, WINDOW, N_KV_HEADS, HEAD_DIM)
    new_v_prompt = kv_tail[:, :, KV_DIM:].reshape(batch, WINDOW, N_KV_HEADS, HEAD_DIM)

    n_rows = dec_batch * dec_seq
    to_tm = lambda a: jnp.swapaxes(a, 0, 1).reshape(a.shape[0] * a.shape[1], a.shape[2])
    from_tm = lambda a, t: jnp.swapaxes(a.reshape(t, dec_batch, a.shape[-1]), 0, 1)
    pos_s = PAST_LEN + jnp.arange(dec_seq, dtype=jnp.int32)
    rope_s = _rope_table(jnp.repeat(pos_s, dec_batch))
    xs = to_tm(x_sample)
    ws_flat = gm_w_s[0][:, :dec_seq, :dec_seq].reshape(-1)
    bs_flat = gm_b_s[0][:, :dec_seq].reshape(-1)
    xs, gm_v = _gmlp(xs, nm, w_in, vn, w_out, tm=n_rows, ws_flat=ws_flat, bs_flat=bs_flat, nt=dec_seq)
    p_s = jnp.swapaxes(p_sample, 1, 2).reshape(depth * n_rows, PLE_DIM)
    state_s = jnp.swapaxes(state_ffn_conv, 1, 2).reshape(depth * (CONV_W - 1) * dec_batch, D_FF)
    xs, conv_s0, kv_s = _ffn(xs, p_s, ffn_w, 0, tm=n_rows, time_steps=dec_seq, state=state_s,
                             kv=(kvn, wkv, rope_s))

    kv_bm = from_tm(kv_s, dec_seq)
    rope_a = _rope_table(jnp.tile(pos_s, dec_batch))
    xa = _attn_sample(from_tm(xs, dec_seq).reshape(n_rows, D_MODEL), nm, wq, rope_a,
                      cache_k_win.reshape(dec_batch, WINDOW, KV_DIM), cache_v_win.reshape(dec_batch, WINDOW, KV_DIM),
                      kv_bm.reshape(n_rows, 2 * KV_DIM), sinks, wo, n_seq=dec_batch, n_new=dec_seq)
    xs = to_tm(xa.reshape(dec_batch, dec_seq, D_MODEL))
    y_s, conv_s1 = _ffn(xs, p_s, ffn_w, 1, tm=n_rows, time_steps=dec_seq, p_offset=1, state=state_s,
                        final_norm=fn)

    y_sample = from_tm(y_s, dec_seq)
    new_gm_v_sample = from_tm(gm_v, dec_seq)[None]
    new_conv_sample = jnp.stack([from_tm(c, CONV_W - 1) for c in (conv_s0, conv_s1)], axis=0)
    new_k_sample = kv_bm[:, :, :KV_DIM].reshape(dec_batch, dec_seq, N_KV_HEADS, HEAD_DIM)
    new_v_sample = kv_bm[:, :, KV_DIM:].reshape(dec_batch, dec_seq, N_KV_HEADS, HEAD_DIM)

    return (y_prompt, y_sample, new_gm_v_sample, new_conv_prompt, new_conv_sample,
            new_k_prompt, new_v_prompt, new_k_sample, new_v_sample)
```

```python
import functools

import numpy as np
import jax
import jax.numpy as jnp
from jax import lax
from jax.experimental import pallas as pl
from jax.experimental.pallas import tpu as pltpu

D_MODEL = 1024
GM_HALF = 2 * D_MODEL
GM_GROUPS = 8
GM_GROUP_DIM = GM_HALF // GM_GROUPS
CHUNK = 128
HEAD_DIM = 64
N_HEADS = D_MODEL // HEAD_DIM
N_KV_HEADS = N_HEADS // 8
KV_DIM = N_KV_HEADS * HEAD_DIM
WINDOW = 128
ROT_DIM = HEAD_DIM // 4
ROT_HALF = ROT_DIM // 2
ROPE_THETA = 500000.0
D_FF = 2816
CONV_W = 3
PLE_DIM = 256
EPS = 1e-6
NEG_INF = -1e30
LOG2_E = 1.4426950408889634
PAST_LEN = 16384

LANES = 128
SUBLANES = 8
MXU_COLS = 256
VMEM_LIMIT_BYTES = 56 * 1024 * 1024
SLABS = D_MODEL // LANES
SLABS_PER_KV = SLABS // N_KV_HEADS
FF_BLOCKS = D_FF // MXU_COLS
SEQ_GROUP = 8
Q_BLOCK_UNROLL = 4

BF16 = jnp.bfloat16
F32 = jnp.float32


def _rms_factor(x):
    ms = jnp.sum(x * x, axis=-1, keepdims=True) * (1.0 / x.shape[-1])
    return lax.rsqrt(ms + EPS)


def _rms(x, w):
    return x * _rms_factor(x) * w


def _dot(a, b):
    return jnp.dot(a, b, preferred_element_type=F32)


def _dot_nt(a, b):
    return lax.dot_general(a, b, (((1,), (1,)), ((), ())), preferred_element_type=F32)


def _gelu(x):
    return jax.nn.gelu(x, approximate=True)


def _rope(x, table):
    c = table[:, 0:LANES]
    s = table[:, LANES:2 * LANES]
    lane = lax.broadcasted_iota(jnp.int32, x.shape, 1)
    partner = jnp.where((lane & ROT_HALF) == 0, pltpu.roll(x, LANES - ROT_HALF, 1), pltpu.roll(x, ROT_HALF, 1))
    return x * c + partner * s


def _const_spec(shape):
    zeros = (0,) * len(shape)
    return pl.BlockSpec(shape, lambda i: zeros, pipeline_mode=pl.Buffered(1))


def _layer_spec(shape, layer):
    idx = (layer,) + (0,) * len(shape)
    return pl.BlockSpec((None,) + tuple(shape), lambda i: idx, pipeline_mode=pl.Buffered(1))


def _row_spec(tm, width, offset=0):
    return pl.BlockSpec((tm, width), lambda i: (i + offset, 0))


_SMEM_SPEC = pl.BlockSpec(memory_space=pltpu.SMEM)


def _params():
    return pltpu.CompilerParams(dimension_semantics=("arbitrary",), vmem_limit_bytes=VMEM_LIMIT_BYTES)


def _gmlp_kernel(*refs, tm, time_major, nt=None):
    if time_major:
        x_ref, nw_ref, win_ref, vn_ref, ws_ref, bs_ref, wout_ref, o_ref, v_ref, v_scr, out_scr = refs
    else:
        x_ref, nw_ref, win_ref, vn_ref, ws_ref, bias_ref, wout_ref, o_ref, v_scr, out_scr = refs

    x = x_ref[...]
    h = _rms(x, nw_ref[...]).astype(BF16)

    ssq = jnp.zeros((tm, 1), F32)
    for g in range(GM_GROUPS):
        sl = slice(g * GM_GROUP_DIM, (g + 1) * GM_GROUP_DIM)
        vg = _gelu(_dot(h, win_ref[:, GM_HALF + g * GM_GROUP_DIM:GM_HALF + (g + 1) * GM_GROUP_DIM]))
        v_scr[:, sl] = vg
        ssq = ssq + jnp.sum(vg * vg, axis=-1, keepdims=True)
    rinv = lax.rsqrt(ssq * (1.0 / GM_HALF) + EPS)

    if not time_major:
        row = lax.broadcasted_iota(jnp.int32, (CHUNK, CHUNK), 0)
        col = lax.broadcasted_iota(jnp.int32, (CHUNK, CHUNK), 1)
        causal = col <= row

    for g in range(GM_GROUPS):
        sl = slice(g * GM_GROUP_DIM, (g + 1) * GM_GROUP_DIM)
        vg = v_scr[:, sl] * rinv * vn_ref[:, sl]
        ug = _gelu(_dot(h, win_ref[:, sl]))
        if time_major:
            v_ref[:, sl] = vg
            nb = tm // nt
            for t in range(nt):
                m = ws_ref[(g * nt + t) * nt] * vg[0:nb]
                for s in range(1, t + 1):
                    m = m + ws_ref[(g * nt + t) * nt + s] * vg[s * nb:(s + 1) * nb]
                m = m + bs_ref[g * nt + t]
                out_scr[t * nb:(t + 1) * nb, sl] = (ug[t * nb:(t + 1) * nb] * m).astype(BF16)
        else:
            vb = vg.astype(BF16)
            wm = jnp.where(causal, ws_ref[g], 0.0).astype(BF16)
            for c in range(tm // CHUNK):
                rs = slice(c * CHUNK, (c + 1) * CHUNK)
                m = _dot(wm, vb[rs]) + bias_ref[:, sl]
                out_scr[rs, sl] = (ug[rs] * m).astype(BF16)

    o_ref[...] = x + _dot(out_scr[...], wout_ref[...])


def _gmlp(x, nw, w_in, vn, w_out, *, tm, ws=None, bias=None, ws_flat=None, bs_flat=None, nt=None):
    rows = x.shape[0]
    time_major = ws is None
    in_specs = [_row_spec(tm, D_MODEL), _layer_spec((1, D_MODEL), 0), _layer_spec((D_MODEL, 2 * GM_HALF), 0),
                _layer_spec((1, GM_HALF), 0)]
    out_specs = [_row_spec(tm, D_MODEL)]
    out_shape = [jax.ShapeDtypeStruct((rows, D_MODEL), F32)]
    if time_major:
        args = [x, nw, w_in, vn, ws_flat, bs_flat, w_out]
        in_specs += [_SMEM_SPEC, _SMEM_SPEC]
        out_specs.append(_row_spec(tm, GM_HALF))
        out_shape.append(jax.ShapeDtypeStruct((rows, GM_HALF), F32))
    else:
        args = [x, nw, w_in, vn, ws, bias, w_out]
        in_specs += [_layer_spec((GM_GROUPS, CHUNK, CHUNK), 0), _const_spec((CHUNK, GM_HALF))]
    in_specs.append(_layer_spec((GM_HALF, D_MODEL), 0))
    return pl.pallas_call(
        functools.partial(_gmlp_kernel, tm=tm, time_major=time_major, nt=nt),
        grid=(rows // tm,),
        in_specs=in_specs,
        out_specs=out_specs,
        out_shape=out_shape,
        scratch_shapes=[pltpu.VMEM((tm, GM_HALF), F32), pltpu.VMEM((tm, GM_HALF), BF16)],
        compiler_params=_params(),
        name="gmlp_sample" if time_major else "gmlp_prompt",
    )(*args)


def _ffn_kernel(*refs, tm, shift, prefix, tiles_per_seq, has_state, with_kv):
    refs = list(refs)
    x_ref, p_ref, nf_ref, wg_ref, wu_ref, cw_ref, cb_ref, wd_ref, pn_ref, pg_ref, pp_ref = refs[:11]
    refs = refs[11:]
    if has_state:
        state_ref = refs.pop(0)
    if with_kv:
        kvn_ref, wkv_ref, rope_ref = refs[:3]
        o_ref, conv_ref, kv_ref, gbuf, act = refs[3:]
    else:
        fn_ref = refs[0]
        o_ref, conv_ref, gbuf, act = refs[1:]

    if has_state:
        gbuf[0:prefix, :] = state_ref[...]
    else:
        @pl.when(pl.program_id(0) % tiles_per_seq == 0)
        def _():
            gbuf[0:prefix, :] = jnp.zeros((prefix, D_FF), F32)

    x = x_ref[...]
    h = _rms(x, nf_ref[...]).astype(BF16)
    for n in range(FF_BLOCKS):
        sl = slice(n * MXU_COLS, (n + 1) * MXU_COLS)
        g = _dot(h, wg_ref[:, sl])
        gbuf[prefix:prefix + tm, sl] = g
        g1 = gbuf[prefix - shift:prefix - shift + tm, sl]
        g2 = gbuf[prefix - 2 * shift:prefix - 2 * shift + tm, sl]
        conv = cb_ref[:, sl] + g2 * cw_ref[0:1, sl]
        conv = conv + g1 * cw_ref[1:2, sl]
        conv = conv + g * cw_ref[2:3, sl]
        up = _dot(h, wu_ref[:, sl])
        act[:, sl] = (_gelu(conv) * up).astype(BF16)

    conv_ref[...] = gbuf[tm:tm + prefix, :]
    if not has_state:
        gbuf[0:prefix, :] = gbuf[tm:tm + prefix, :]

    x = x + _dot(act[...], wd_ref[...])
    gate = jax.nn.sigmoid(_dot((x * pn_ref[...]).astype(BF16), pg_ref[...]) * _rms_factor(x))
    x = x + _dot(p_ref[...].astype(BF16), pp_ref[...]) * gate

    if with_kv:
        o_ref[...] = x
        kv = _dot((x * kvn_ref[...]).astype(BF16), wkv_ref[...]) * _rms_factor(x)
        kv_ref[:, 0:KV_DIM] = _rope(kv[:, 0:KV_DIM], rope_ref[...])
        kv_ref[:, KV_DIM:2 * KV_DIM] = kv[:, KV_DIM:2 * KV_DIM]
    else:
        o_ref[...] = _rms(x, fn_ref[...])


def _ffn(x, p, w, layer, *, tm, seq=None, p_offset=0, time_steps=None, state=None, kv=None, final_norm=None):
    rows = x.shape[0]
    has_state = state is not None
    with_kv = kv is not None
    time_major = time_steps is not None
    if time_major:
        shift = rows // time_steps
        prefix, tiles_per_seq = (CONV_W - 1) * shift, 1
        conv_rows, conv_spec = prefix, pl.BlockSpec((prefix, D_FF), lambda i: (0, 0))
    else:
        shift, prefix = 1, SUBLANES
        tiles_per_seq = seq // tm
        conv_rows = (rows // seq) * prefix
        conv_spec = pl.BlockSpec((prefix, D_FF), lambda i: (i // tiles_per_seq, 0))

    args = [x, p, w["nf"], w["wg"], w["wu"], w["cw"], w["cb"], w["wd"], w["pn"], w["pg"], w["pp"]]
    in_specs = [
        _row_spec(tm, D_MODEL), _row_spec(tm, PLE_DIM, p_offset), _layer_spec((1, D_MODEL), layer),
        _layer_spec((D_MODEL, D_FF), layer), _layer_spec((D_MODEL, D_FF), layer),
        _layer_spec((CONV_W, D_FF), layer), _layer_spec((1, D_FF), layer), _layer_spec((D_FF, D_MODEL), layer),
        _layer_spec((1, D_MODEL), layer), _layer_spec((D_MODEL, D_MODEL), layer),
        _layer_spec((PLE_DIM, D_MODEL), layer),
    ]
    scratch = [pltpu.VMEM((prefix + tm, D_FF), F32), pltpu.VMEM((tm, D_FF), BF16)]
    if has_state:
        args.append(state)
        in_specs.append(pl.BlockSpec((prefix, D_FF), lambda i: (layer, 0), pipeline_mode=pl.Buffered(1)))
    out_specs = [_row_spec(tm, D_MODEL), conv_spec]
    out_shape = [jax.ShapeDtypeStruct((rows, D_MODEL), F32), jax.ShapeDtypeStruct((conv_rows, D_FF), F32)]
    if with_kv:
        kvn, wkv, rope = kv
        args += [kvn, wkv, rope]
        rope_tiles = rope.shape[0] // tm
        in_specs += [_const_spec((1, D_MODEL)), _const_spec((D_MODEL, 2 * KV_DIM)),
                     pl.BlockSpec((tm, 2 * LANES), lambda i: (i % rope_tiles, 0))]
        out_specs.append(_row_spec(tm, 2 * KV_DIM))
        out_shape.append(jax.ShapeDtypeStruct((rows, 2 * KV_DIM), F32))
    else:
        args.append(final_norm)
        in_specs.append(_const_spec((1, D_MODEL)))

    return pl.pallas_call(
        functools.partial(_ffn_kernel, tm=tm, shift=shift, prefix=prefix, tiles_per_seq=tiles_per_seq,
                          has_state=has_state, with_kv=with_kv),
        grid=(rows // tm,),
        in_specs=in_specs,
        out_specs=out_specs,
        out_shape=out_shape,
        scratch_shapes=scratch,
        compiler_params=_params(),
        name=("ffn_kv" if with_kv else "ffn_final") + ("_sample" if time_major else "_prompt"),
    )(*args)


def _q_proj(x_ref, nm_ref, wq_ref, rope_ref, q_scr):
    h = _rms(x_ref[...], nm_ref[...]).astype(BF16)
    table = rope_ref[...]
    for c in range(D_MODEL // MXU_COLS):
        qc = _dot(h, wq_ref[:, c * MXU_COLS:(c + 1) * MXU_COLS])
        for s in range(MXU_COLS // LANES):
            col = c * MXU_COLS + s * LANES
            qs = _rope(qc[:, s * LANES:(s + 1) * LANES], table) * (HEAD_DIM ** -0.5 * LOG2_E)
            q_scr[:, col:col + LANES] = qs.astype(q_scr.dtype)


def _head_frames(kv):
    lane = lax.broadcasted_iota(jnp.int32, kv.shape, 1)
    lo = lane < HEAD_DIM
    sw = pltpu.roll(kv, HEAD_DIM, 1)
    zero = jnp.zeros_like(kv)
    return [jnp.where(lo, kv, zero), jnp.where(lo, zero, sw), jnp.where(lo, sw, zero), jnp.where(lo, zero, kv)]


def _softmax_parts(s, mask, sink):
    col0 = lax.broadcasted_iota(jnp.int32, (s.shape[0], LANES), 1) == 0
    s = jnp.where(mask, s, NEG_INF)
    s = jnp.concatenate([jnp.where(col0, sink * LOG2_E, s[:, 0:LANES]), s[:, LANES:]], axis=1)
    e = jnp.exp2(s - jnp.max(s, axis=-1, keepdims=True))
    den = jnp.sum(e, axis=-1, keepdims=True)
    e = jnp.concatenate([jnp.where(col0, 0.0, e[:, 0:LANES]), e[:, LANES:]], axis=1)
    return e, 1.0 / den


def _attend(qstack, k_lo, k_hi, v_lo, v_hi, mask, sinks_ref, first_slab, rows):
    weights, scales = [], []
    for par, k_fr in enumerate((k_lo, k_hi)):
        sc = _dot_nt(qstack, k_fr)
        w_par, r_par = [], []
        for s in range(SLABS_PER_KV):
            head = (first_slab + s) * 2 + par
            e, r = _softmax_parts(sc[s * rows:(s + 1) * rows], mask, sinks_ref[head])
            w_par.append(e.astype(BF16))
            r_par.append(r)
        weights.append(w_par)
        scales.append(r_par)
    vcat = jnp.concatenate([v_lo, v_hi], axis=0)
    wcat = jnp.concatenate(
        [jnp.concatenate([weights[0][s], weights[1][s]], axis=1) for s in range(SLABS_PER_KV)], axis=0)
    o = _dot(wcat, vcat)
    lane = lax.broadcasted_iota(jnp.int32, (rows, LANES), 1)
    return [o[s * rows:(s + 1) * rows] * jnp.where(lane < HEAD_DIM, scales[0][s], scales[1][s])
            for s in range(SLABS_PER_KV)]


def _attn_prompt_kernel(x_ref, nm_ref, wq_ref, rope_ref, kvc_ref, kvp_ref, sinks_ref, wo_ref,
                        o_ref, q_scr, kx, vx, a_scr, *, tm, tiles_per_seq):
    first_tile = pl.program_id(0) % tiles_per_seq == 0
    _q_proj(x_ref, nm_ref, wq_ref, rope_ref, q_scr)

    kv = jnp.concatenate([kvp_ref[...], kvc_ref[...]], axis=0)
    for f, fr in enumerate(_head_frames(kv[:, 0:KV_DIM])):
        kx[f] = fr.astype(BF16)
    for f, fr in enumerate(_head_frames(kv[:, KV_DIM:2 * KV_DIM])):
        vx[f] = fr.astype(BF16)

    qi = lax.broadcasted_iota(jnp.int32, (WINDOW, 2 * WINDOW), 0)
    sj = lax.broadcasted_iota(jnp.int32, (WINDOW, 2 * WINDOW), 1)
    band = (sj > qi) & (sj <= qi + WINDOW)

    def q_block(qb, carry):
        r0 = pl.multiple_of(qb * WINDOW, WINDOW)
        lo_lim = jnp.where(jnp.logical_and(first_tile, qb == 0), WINDOW, 0)
        mask = band & (sj >= lo_lim)
        for j in range(N_KV_HEADS):
            first_slab = SLABS_PER_KV * j
            qstack = jnp.concatenate(
                [q_scr[pl.ds(r0, WINDOW), (first_slab + s) * LANES:(first_slab + s + 1) * LANES]
                 for s in range(SLABS_PER_KV)], axis=0)
            keys = pl.ds(r0, 2 * WINDOW)
            outs = _attend(qstack, kx[2 * j, keys, :], kx[2 * j + 1, keys, :], vx[2 * j, keys, :],
                           vx[2 * j + 1, keys, :], mask, sinks_ref, first_slab, WINDOW)
            for s, o in enumerate(outs):
                col = (first_slab + s) * LANES
                a_scr[pl.ds(r0, WINDOW), col:col + LANES] = o.astype(BF16)
        return carry

    lax.fori_loop(0, tm // WINDOW, q_block, 0, unroll=Q_BLOCK_UNROLL)
    o_ref[...] = x_ref[...] + _dot(a_scr[...], wo_ref[...])


def _attn_prompt(x, nm, wq, rope, kv, sinks, wo, *, tm, seq):
    rows = x.shape[0]
    tiles_per_seq = seq // tm
    blocks_per_tile = tm // WINDOW
    cur = pl.BlockSpec((tm, 2 * KV_DIM), lambda i: (i, 0))
    prev = pl.BlockSpec((WINDOW, 2 * KV_DIM), lambda i: (jnp.maximum(i * blocks_per_tile - 1, 0), 0))
    return pl.pallas_call(
        functools.partial(_attn_prompt_kernel, tm=tm, tiles_per_seq=tiles_per_seq),
        grid=(rows // tm,),
        in_specs=[
            _row_spec(tm, D_MODEL), _layer_spec((1, D_MODEL), 1), _layer_spec((D_MODEL, D_MODEL), 0),
            pl.BlockSpec((tm, 2 * LANES), lambda i: (i % tiles_per_seq, 0)),
            cur, prev, _SMEM_SPEC, _layer_spec((D_MODEL, D_MODEL), 0),
        ],
        out_specs=_row_spec(tm, D_MODEL),
        out_shape=jax.ShapeDtypeStruct((rows, D_MODEL), F32),
        scratch_shapes=[
            pltpu.VMEM((tm, D_MODEL), BF16),
            pltpu.VMEM((4, WINDOW + tm, LANES), BF16),
            pltpu.VMEM((4, WINDOW + tm, LANES), BF16),
            pltpu.VMEM((tm, D_MODEL), BF16),
        ],
        compiler_params=_params(),
        name="attn_prompt",
    )(x, nm, wq, rope, kv, kv, sinks, wo)


def _attn_sample_kernel(x_ref, nm_ref, wq_ref, rope_ref, ck_ref, cv_ref, kvn_ref, sinks_ref, wo_ref,
                        o_ref, q_scr, a_scr, *, n_seq, n_new):
    _q_proj(x_ref, nm_ref, wq_ref, rope_ref, q_scr)

    rows = SEQ_GROUP * n_new
    n_cached = SEQ_GROUP * WINDOW
    frame = n_cached + LANES
    r = lax.broadcasted_iota(jnp.int32, (rows, frame), 0)
    c = lax.broadcasted_iota(jnp.int32, (rows, frame), 1)
    t_bits, w_bits = n_new.bit_length() - 1, WINDOW.bit_length() - 1
    q_seq, q_t = r >> t_bits, r & (n_new - 1)
    fresh = c - n_cached
    seen_cached = ((c >> w_bits) == q_seq) & ((c & (WINDOW - 1)) > q_t)
    seen_fresh = ((fresh >> t_bits) == q_seq) & ((fresh & (n_new - 1)) <= q_t)
    mask = seen_cached | seen_fresh
    fresh_pad = jnp.zeros((LANES - rows, KV_DIM), F32)

    def seq_group(g, carry):
        r0 = pl.multiple_of(g * rows, rows)
        seqs = pl.ds(g * SEQ_GROUP, SEQ_GROUP)
        kv_new = kvn_ref[pl.ds(r0, rows), :]
        k_fr = [f.astype(BF16) for f in _head_frames(jnp.concatenate(
            [ck_ref[seqs].reshape(n_cached, KV_DIM), kv_new[:, 0:KV_DIM], fresh_pad], axis=0))]
        v_fr = [f.astype(BF16) for f in _head_frames(jnp.concatenate(
            [cv_ref[seqs].reshape(n_cached, KV_DIM), kv_new[:, KV_DIM:2 * KV_DIM], fresh_pad], axis=0))]
        for j in range(N_KV_HEADS):
            first_slab = SLABS_PER_KV * j
            qstack = jnp.concatenate(
                [q_scr[pl.ds(r0, rows), (first_slab + s) * LANES:(first_slab + s + 1) * LANES]
                 for s in range(SLABS_PER_KV)], axis=0)
            outs = _attend(qstack, k_fr[2 * j], k_fr[2 * j + 1], v_fr[2 * j], v_fr[2 * j + 1], mask, sinks_ref,
                           first_slab, rows)
            for s, o in enumerate(outs):
                col = (first_slab + s) * LANES
                a_scr[pl.ds(r0, rows), col:col + LANES] = o.astype(BF16)
        return carry

    lax.fori_loop(0, n_seq // SEQ_GROUP, seq_group, 0)
    o_ref[...] = x_ref[...] + _dot(a_scr[...], wo_ref[...])


def _attn_sample(x, nm, wq, rope, ck, cv, kvn, sinks, wo, *, n_seq, n_new):
    rows = x.shape[0]
    full = lambda shape: pl.BlockSpec(shape, lambda i: (0,) * len(shape))
    return pl.pallas_call(
        functools.partial(_attn_sample_kernel, n_seq=n_seq, n_new=n_new),
        grid=(1,),
        in_specs=[
            full((rows, D_MODEL)), _layer_spec((1, D_MODEL), 1), _layer_spec((D_MODEL, D_MODEL), 0),
            full((rows, 2 * LANES)), full(ck.shape), full(cv.shape), full(kvn.shape),
            _SMEM_SPEC, _layer_spec((D_MODEL, D_MODEL), 0),
        ],
        out_specs=full((rows, D_MODEL)),
        out_shape=jax.ShapeDtypeStruct((rows, D_MODEL), F32),
        scratch_shapes=[pltpu.VMEM((rows, D_MODEL), BF16), pltpu.VMEM((rows, D_MODEL), BF16)],
        compiler_params=_params(),
        name="attn_sample",
    )(x, nm, wq, rope, ck, cv, kvn, sinks, wo)


def _rope_table(pos):
    dim = np.arange(LANES) % HEAD_DIM
    inv_freq = ROPE_THETA ** (-jnp.arange(ROT_HALF, dtype=F32) / ROT_HALF)
    ang = pos.astype(F32)[:, None] * inv_freq[dim % ROT_HALF][None, :]
    in_rot = jnp.asarray(dim < ROT_DIM)[None, :]
    first = jnp.asarray(dim < ROT_HALF)[None, :]
    sin = jnp.sin(ang)
    c = jnp.where(in_rot, jnp.cos(ang), 1.0)
    s = jnp.where(first, -sin, jnp.where(in_rot, sin, 0.0))
    return jnp.concatenate([c, s], axis=1)


def kernel(x_prompt, x_sample, state_ffn_conv, cache_k_win, cache_v_win, p_prompt, p_sample, norm_mix, gm_w_in,
           gm_v_norm, gm_w_s, gm_b_s, gm_w_out, kv_norm, w_kv, w_q, attn_sinks, w_o, norm_ffn, ffn_w_gate,
           ffn_w_up, ffn_conv_w, ffn_conv_b, ffn_w_down, ple_norm, ple_w_gate, ple_w_proj, final_norm):
    batch, seq, _ = x_prompt.shape
    dec_batch, dec_seq, _ = x_sample.shape
    depth = norm_mix.shape[0]
    assert depth == 2 and gm_w_in.shape[0] == 1 and w_q.shape[0] == 1
    tm, tm_ffn = 1024, 512
    assert seq % tm == 0 and seq % tm_ffn == 0 and dec_batch % SEQ_GROUP == 0
    assert dec_seq & (dec_seq - 1) == 0 and (SEQ_GROUP * dec_seq) % (2 * SUBLANES) == 0
    assert SEQ_GROUP * dec_seq <= LANES

    rows3 = lambda a: a.reshape(a.shape[0], 1, a.shape[1])
    ffn_w = dict(
        nf=rows3(norm_ffn), wg=ffn_w_gate.astype(BF16), wu=ffn_w_up.astype(BF16), cw=ffn_conv_w,
        cb=rows3(ffn_conv_b), wd=ffn_w_down.astype(BF16), pn=rows3(ple_norm), pg=ple_w_gate.astype(BF16),
        pp=ple_w_proj.astype(BF16))
    w_in = gm_w_in.astype(BF16)
    w_out = gm_w_out.astype(BF16)
    wq = w_q.astype(BF16)
    wo = w_o.astype(BF16)
    wkv = w_kv.astype(BF16)
    kvn = kv_norm.reshape(1, D_MODEL)
    nm = rows3(norm_mix)
    vn = rows3(gm_v_norm)
    fn = final_norm.reshape(1, D_MODEL)
    sinks = attn_sinks[0]

    n_rows_p = batch * seq
    rope_p = _rope_table(jnp.arange(seq, dtype=jnp.int32))
    bias_p = jnp.repeat(gm_b_s[0].T, GM_GROUP_DIM, axis=1)
    xp = x_prompt.reshape(n_rows_p, D_MODEL)
    p_p = p_prompt.reshape(depth * n_rows_p, PLE_DIM)
    (xp,) = _gmlp(xp, nm, w_in, vn, w_out, tm=tm, ws=gm_w_s, bias=bias_p)
    xp, conv_p0, kv_p = _ffn(xp, p_p, ffn_w, 0, tm=tm_ffn, seq=seq, kv=(kvn, wkv, rope_p))
    xp = _attn_prompt(xp, nm, wq, rope_p, kv_p, sinks, wo, tm=tm, seq=seq)
    y_p, conv_p1 = _ffn(xp, p_p, ffn_w, 1, tm=tm_ffn, seq=seq, p_offset=n_rows_p // tm_ffn, final_norm=fn)

    y_prompt = y_p.reshape(batch, seq, D_MODEL)
    new_conv_prompt = jnp.stack(
        [c.reshape(batch, SUBLANES, D_FF)[:, SUBLANES - (CONV_W - 1):] for c in (conv_p0, conv_p1)], axis=0)
    kv_tail = kv_p.reshape(batch, seq, 2 * KV_DIM)[:, seq - WINDOW:]
    new_k_prompt = kv_tail[:, :, :KV_DIM].reshape(batch, WINDOW, N_KV_HEADS, HEAD_DIM)
    new_v_prompt = kv_tail[:, :, KV_DIM:].reshape(batch, WINDOW, N_KV_HEADS, HEAD_DIM)

    n_rows = dec_batch * dec_seq
    to_tm = lambda a: jnp.swapaxes(a, 0, 1).reshape(a.shape[0] * a.shape[1], a.shape[2])
    from_tm = lambda a, t: jnp.swapaxes(a.reshape(t, dec_batch, a.shape[-1]), 0, 1)
    pos_s = PAST_LEN + jnp.arange(dec_seq, dtype=jnp.int32)
    rope_s = _rope_table(jnp.repeat(pos_s, dec_batch))
    xs = to_tm(x_sample)
    ws_flat = gm_w_s[0][:, :dec_seq, :dec_seq].reshape(-1)
    bs_flat = gm_b_s[0][:, :dec_seq].reshape(-1)
    xs, gm_v = _gmlp(xs, nm, w_in, vn, w_out, tm=n_rows, ws_flat=ws_flat, bs_flat=bs_flat, nt=dec_seq)
    p_s = jnp.swapaxes(p_sample, 1, 2).reshape(depth * n_rows, PLE_DIM)
    state_s = jnp.swapaxes(state_ffn_conv, 1, 2).reshape(depth * (CONV_W - 1) * dec_batch, D_FF)
    xs, conv_s0, kv_s = _ffn(xs, p_s, ffn_w, 0, tm=n_rows, time_steps=dec_seq, state=state_s,
                             kv=(kvn, wkv, rope_s))

    kv_bm = from_tm(kv_s, dec_seq)
    rope_a = _rope_table(jnp.tile(pos_s, dec_batch))
    xa = _attn_sample(from_tm(xs, dec_seq).reshape(n_rows, D_MODEL), nm, wq, rope_a,
                      cache_k_win.reshape(dec_batch, WINDOW, KV_DIM), cache_v_win.reshape(dec_batch, WINDOW, KV_DIM),
                      kv_bm.reshape(n_rows, 2 * KV_DIM), sinks, wo, n_seq=dec_batch, n_new=dec_seq)
    xs = to_tm(xa.reshape(dec_batch, dec_seq, D_MODEL))
    y_s, conv_s1 = _ffn(xs, p_s, ffn_w, 1, tm=n_rows, time_steps=dec_seq, p_offset=1, state=state_s,
                        final_norm=fn)

    y_sample = from_tm(y_s, dec_seq)
    new_gm_v_sample = from_tm(gm_v, dec_seq)[None]
    new_conv_sample = jnp.stack([from_tm(c, CONV_W - 1) for c in (conv_s0, conv_s1)], axis=0)
    new_k_sample = kv_bm[:, :, :KV_DIM].reshape(dec_batch, dec_seq, N_KV_HEADS, HEAD_DIM)
    new_v_sample = kv_bm[:, :, KV_DIM:].reshape(dec_batch, dec_seq, N_KV_HEADS, HEAD_DIM)

    return (y_prompt, y_sample, new_gm_v_sample, new_conv_prompt, new_conv_sample,
            new_k_prompt, new_v_prompt, new_k_sample, new_v_sample)
```

```python
import functools

import numpy as np
import jax
import jax.numpy as jnp
from jax import lax
from jax.experimental import pallas as pl
from jax.experimental.pallas import tpu as pltpu

D_MODEL = 1024
GM_HALF = 2 * D_MODEL
GM_GROUPS = 8
GM_GROUP_DIM = GM_HALF // GM_GROUPS
CHUNK = 128
HEAD_DIM = 64
N_HEADS = D_MODEL // HEAD_DIM
N_KV_HEADS = N_HEADS // 8
KV_DIM = N_KV_HEADS * HEAD_DIM
WINDOW = 128
ROT_DIM = HEAD_DIM // 4
ROT_HALF = ROT_DIM // 2
ROPE_THETA = 500000.0
D_FF = 2816
CONV_W = 3
PLE_DIM = 256
EPS = 1e-6
NEG_INF = -1e30
LOG2_E = 1.4426950408889634
PAST_LEN = 16384

LANES = 128
SUBLANES = 8
MXU_COLS = 256
VMEM_LIMIT_BYTES = 56 * 1024 * 1024
SLABS = D_MODEL // LANES
SLABS_PER_KV = SLABS // N_KV_HEADS
FF_BLOCKS = D_FF // MXU_COLS
SEQ_GROUP = 8
Q_BLOCK_UNROLL = 4

BF16 = jnp.bfloat16
F32 = jnp.float32


def _rms_factor(x):
    ms = jnp.sum(x * x, axis=-1, keepdims=True) * (1.0 / x.shape[-1])
    return lax.rsqrt(ms + EPS)


def _rms(x, w):
    return x * _rms_factor(x) * w


def _dot(a, b):
    return jnp.dot(a, b, preferred_element_type=F32)


def _dot_nt(a, b):
    return lax.dot_general(a, b, (((1,), (1,)), ((), ())), preferred_element_type=F32)


def _gelu(x):
    return jax.nn.gelu(x, approximate=True)


def _rope(x, table):
    c = table[:, 0:LANES]
    s = table[:, LANES:2 * LANES]
    lane = lax.broadcasted_iota(jnp.int32, x.shape, 1)
    partner = jnp.where((lane & ROT_HALF) == 0, pltpu.roll(x, LANES - ROT_HALF, 1), pltpu.roll(x, ROT_HALF, 1))
    return x * c + partner * s


def _const_spec(shape):
    zeros = (0,) * len(shape)
    return pl.BlockSpec(shape, lambda i: zeros, pipeline_mode=pl.Buffered(1))


def _layer_spec(shape, layer):
    idx = (layer,) + (0,) * len(shape)
    return pl.BlockSpec((None,) + tuple(shape), lambda i: idx, pipeline_mode=pl.Buffered(1))


def _row_spec(tm, width, offset=0):
    return pl.BlockSpec((tm, width), lambda i: (i + offset, 0))


_SMEM_SPEC = pl.BlockSpec(memory_space=pltpu.SMEM)


def _params():
    return pltpu.CompilerParams(dimension_semantics=("arbitrary",), vmem_limit_bytes=VMEM_LIMIT_BYTES)


def _gmlp_kernel(*refs, tm, time_major, nt=None):
    if time_major:
        x_ref, nw_ref, win_ref, vn_ref, ws_ref, bs_ref, wout_ref, o_ref, v_ref, v_scr, out_scr = refs
    else:
        x_ref, nw_ref, win_ref, vn_ref, ws_ref, bias_ref, wout_ref, o_ref, v_scr, out_scr = refs

    x = x_ref[...]
    h = _rms(x, nw_ref[...]).astype(BF16)

    ssq = jnp.zeros((tm, 1), F32)
    for g in range(GM_GROUPS):
        sl = slice(g * GM_GROUP_DIM, (g + 1) * GM_GROUP_DIM)
        vg = _gelu(_dot(h, win_ref[:, GM_HALF + g * GM_GROUP_DIM:GM_HALF + (g + 1) * GM_GROUP_DIM]))
        v_scr[:, sl] = vg
        ssq = ssq + jnp.sum(vg * vg, axis=-1, keepdims=True)
    rinv = lax.rsqrt(ssq * (1.0 / GM_HALF) + EPS)

    if not time_major:
        row = lax.broadcasted_iota(jnp.int32, (CHUNK, CHUNK), 0)
        col = lax.broadcasted_iota(jnp.int32, (CHUNK, CHUNK), 1)
        causal = col <= row

    for g in range(GM_GROUPS):
        sl = slice(g * GM_GROUP_DIM, (g + 1) * GM_GROUP_DIM)
        vg = v_scr[:, sl] * rinv * vn_ref[:, sl]
        ug = _gelu(_dot(h, win_ref[:, sl]))
        if time_major:
            v_ref[:, sl] = vg
            nb = tm // nt
            for t in range(nt):
                m = ws_ref[(g * nt + t) * nt] * vg[0:nb]
                for s in range(1, t + 1):
                    m = m + ws_ref[(g * nt + t) * nt + s] * vg[s * nb:(s + 1) * nb]
                m = m + bs_ref[g * nt + t]
                out_scr[t * nb:(t + 1) * nb, sl] = (ug[t * nb:(t + 1) * nb] * m).astype(BF16)
        else:
            vb = vg.astype(BF16)
            wm = jnp.where(causal, ws_ref[g], 0.0).astype(BF16)
            for c in range(tm // CHUNK):
                rs = slice(c * CHUNK, (c + 1) * CHUNK)
                m = _dot(wm, vb[rs]) + bias_ref[:, sl]
                out_scr[rs, sl] = (ug[rs] * m).astype(BF16)

    o_ref[...] = x_ref[...] + _dot(out_scr[...], wout_ref[...])


def _gmlp(x, nw, w_in, vn, w_out, *, tm, ws=None, bias=None, ws_flat=None, bs_flat=None, nt=None):
    rows = x.shape[0]
    time_major = ws is None
    in_specs = [_row_spec(tm, D_MODEL), _layer_spec((1, D_MODEL), 0), _layer_spec((D_MODEL, 2 * GM_HALF), 0),
                _layer_spec((1, GM_HALF), 0)]
    out_specs = [_row_spec(tm, D_MODEL)]
    out_shape = [jax.ShapeDtypeStruct((rows, D_MODEL), F32)]
    if time_major:
        args = [x, nw, w_in, vn, ws_flat, bs_flat, w_out]
        in_specs += [_SMEM_SPEC, _SMEM_SPEC]
        out_specs.append(_row_spec(tm, GM_HALF))
        out_shape.append(jax.ShapeDtypeStruct((rows, GM_HALF), F32))
    else:
        args = [x, nw, w_in, vn, ws, bias, w_out]
        in_specs += [_layer_spec((GM_GROUPS, CHUNK, CHUNK), 0), _const_spec((CHUNK, GM_HALF))]
    in_specs.append(_layer_spec((GM_HALF, D_MODEL), 0))
    return pl.pallas_call(
        functools.partial(_gmlp_kernel, tm=tm, time_major=time_major, nt=nt),
        grid=(rows // tm,),
        in_specs=in_specs,
        out_specs=out_specs,
        out_shape=out_shape,
        scratch_shapes=[pltpu.VMEM((tm, GM_HALF), F32), pltpu.VMEM((tm, GM_HALF), BF16)],
        compiler_params=_params(),
        name="gmlp_sample" if time_major else "gmlp_prompt",
    )(*args)


def _ffn_kernel(*refs, tm, shift, prefix, tiles_per_seq, has_state, with_kv):
    refs = list(refs)
    x_ref, p_ref, nf_ref, wg_ref, wu_ref, cw_ref, cb_ref, wd_ref, pn_ref, pg_ref, pp_ref = refs[:11]
    refs = refs[11:]
    if has_state:
        state_ref = refs.pop(0)
    if with_kv:
        kvn_ref, wkv_ref, rope_ref = refs[:3]
        o_ref, conv_ref, kv_ref, gbuf, act = refs[3:]
    else:
        fn_ref = refs[0]
        o_ref, conv_ref, gbuf, act = refs[1:]

    if has_state:
        gbuf[0:prefix, :] = state_ref[...]
    else:
        @pl.when(pl.program_id(0) % tiles_per_seq == 0)
        def _():
            gbuf[0:prefix, :] = jnp.zeros((prefix, D_FF), F32)

    x = x_ref[...]
    h = _rms(x, nf_ref[...]).astype(BF16)
    for n in range(FF_BLOCKS):
        sl = slice(n * MXU_COLS, (n + 1) * MXU_COLS)
        g = _dot(h, wg_ref[:, sl])
        gbuf[prefix:prefix + tm, sl] = g
        g1 = gbuf[prefix - shift:prefix - shift + tm, sl]
        g2 = gbuf[prefix - 2 * shift:prefix - 2 * shift + tm, sl]
        conv = cb_ref[:, sl] + g2 * cw_ref[0:1, sl]
        conv = conv + g1 * cw_ref[1:2, sl]
        conv = conv + g * cw_ref[2:3, sl]
        up = _dot(h, wu_ref[:, sl])
        act[:, sl] = (_gelu(conv) * up).astype(BF16)

    conv_ref[...] = gbuf[tm:tm + prefix, :]
    if not has_state:
        gbuf[0:prefix, :] = gbuf[tm:tm + prefix, :]

    x = x_ref[...] + _dot(act[...], wd_ref[...])
    gate = jax.nn.sigmoid(_dot((x * pn_ref[...]).astype(BF16), pg_ref[...]) * _rms_factor(x))
    x = x + _dot(p_ref[...].astype(BF16), pp_ref[...]) * gate

    if with_kv:
        o_ref[...] = x
        kv = _dot((x * kvn_ref[...]).astype(BF16), wkv_ref[...]) * _rms_factor(x)
        kv_ref[:, 0:KV_DIM] = _rope(kv[:, 0:KV_DIM], rope_ref[...])
        kv_ref[:, KV_DIM:2 * KV_DIM] = kv[:, KV_DIM:2 * KV_DIM]
    else:
        o_ref[...] = _rms(x, fn_ref[...])


def _ffn(x, p, w, layer, *, tm, seq=None, p_offset=0, time_steps=None, state=None, kv=None, final_norm=None):
    rows = x.shape[0]
    has_state = state is not None
    with_kv = kv is not None
    time_major = time_steps is not None
    if time_major:
        shift = rows // time_steps
        prefix, tiles_per_seq = (CONV_W - 1) * shift, 1
        conv_rows, conv_spec = prefix, pl.BlockSpec((prefix, D_FF), lambda i: (0, 0))
    else:
        shift, prefix = 1, SUBLANES
        tiles_per_seq = seq // tm
        conv_rows = (rows // seq) * prefix
        conv_spec = pl.BlockSpec((prefix, D_FF), lambda i: (i // tiles_per_seq, 0))

    args = [x, p, w["nf"], w["wg"], w["wu"], w["cw"], w["cb"], w["wd"], w["pn"], w["pg"], w["pp"]]
    in_specs = [
        _row_spec(tm, D_MODEL), _row_spec(tm, PLE_DIM, p_offset), _layer_spec((1, D_MODEL), layer),
        _layer_spec((D_MODEL, D_FF), layer), _layer_spec((D_MODEL, D_FF), layer),
        _layer_spec((CONV_W, D_FF), layer), _layer_spec((1, D_FF), layer), _layer_spec((D_FF, D_MODEL), layer),
        _layer_spec((1, D_MODEL), layer), _layer_spec((D_MODEL, D_MODEL), layer),
        _layer_spec((PLE_DIM, D_MODEL), layer),
    ]
    scratch = [pltpu.VMEM((prefix + tm, D_FF), F32), pltpu.VMEM((tm, D_FF), BF16)]
    if has_state:
        args.append(state)
        in_specs.append(pl.BlockSpec((prefix, D_FF), lambda i: (layer, 0), pipeline_mode=pl.Buffered(1)))
    out_specs = [_row_spec(tm, D_MODEL), conv_spec]
    out_shape = [jax.ShapeDtypeStruct((rows, D_MODEL), F32), jax.ShapeDtypeStruct((conv_rows, D_FF), F32)]
    if with_kv:
        kvn, wkv, rope = kv
        args += [kvn, wkv, rope]
        rope_tiles = rope.shape[0] // tm
        in_specs += [_const_spec((1, D_MODEL)), _const_spec((D_MODEL, 2 * KV_DIM)),
                     pl.BlockSpec((tm, 2 * LANES), lambda i: (i % rope_tiles, 0))]
        out_specs.append(_row_spec(tm, 2 * KV_DIM))
        out_shape.append(jax.ShapeDtypeStruct((rows, 2 * KV_DIM), F32))
    else:
        args.append(final_norm)
        in_specs.append(_const_spec((1, D_MODEL)))

    return pl.pallas_call(
        functools.partial(_ffn_kernel, tm=tm, shift=shift, prefix=prefix, tiles_per_seq=tiles_per_seq,
                          has_state=has_state, with_kv=with_kv),
        grid=(rows // tm,),
        in_specs=in_specs,
        out_specs=out_specs,
        out_shape=out_shape,
        scratch_shapes=scratch,
        compiler_params=_params(),
        name=("ffn_kv" if with_kv else "ffn_final") + ("_sample" if time_major else "_prompt"),
    )(*args)


def _q_proj(x_ref, nm_ref, wq_ref, rope_ref, q_scr):
    h = _rms(x_ref[...], nm_ref[...]).astype(BF16)
    table = rope_ref[...]
    for c in range(D_MODEL // MXU_COLS):
        qc = _dot(h, wq_ref[:, c * MXU_COLS:(c + 1) * MXU_COLS])
        for s in range(MXU_COLS // LANES):
            col = c * MXU_COLS + s * LANES
            qs = _rope(qc[:, s * LANES:(s + 1) * LANES], table) * (HEAD_DIM ** -0.5 * LOG2_E)
            q_scr[:, col:col + LANES] = qs.astype(q_scr.dtype)


def _head_frames(kv):
    lane = lax.broadcasted_iota(jnp.int32, kv.shape, 1)
    lo = lane < HEAD_DIM
    sw = pltpu.roll(kv, HEAD_DIM, 1)
    zero = jnp.zeros_like(kv)
    return [jnp.where(lo, kv, zero), jnp.where(lo, zero, sw), jnp.where(lo, sw, zero), jnp.where(lo, zero, kv)]


def _softmax_parts(s, mask, sink):
    col0 = lax.broadcasted_iota(jnp.int32, (s.shape[0], LANES), 1) == 0
    s = jnp.where(mask, s, NEG_INF)
    s = jnp.concatenate([jnp.where(col0, sink * LOG2_E, s[:, 0:LANES]), s[:, LANES:]], axis=1)
    e = jnp.exp2(s - jnp.max(s, axis=-1, keepdims=True))
    den = jnp.sum(e, axis=-1, keepdims=True)
    e = jnp.concatenate([jnp.where(col0, 0.0, e[:, 0:LANES]), e[:, LANES:]], axis=1)
    return e, 1.0 / den


def _attend(qstack, k_lo, k_hi, v_lo, v_hi, mask, sinks_ref, first_slab, rows):
    weights, scales = [], []
    for par, k_fr in enumerate((k_lo, k_hi)):
        sc = _dot_nt(qstack, k_fr)
        w_par, r_par = [], []
        for s in range(SLABS_PER_KV):
            head = (first_slab + s) * 2 + par
            e, r = _softmax_parts(sc[s * rows:(s + 1) * rows], mask, sinks_ref[head])
            w_par.append(e.astype(BF16))
            r_par.append(r)
        weights.append(w_par)
        scales.append(r_par)
    vcat = jnp.concatenate([v_lo, v_hi], axis=0)
    wcat = jnp.concatenate(
        [jnp.concatenate([weights[0][s], weights[1][s]], axis=1) for s in range(SLABS_PER_KV)], axis=0)
    o = _dot(wcat, vcat)
    lane = lax.broadcasted_iota(jnp.int32, (rows, LANES), 1)
    return [o[s * rows:(s + 1) * rows] * jnp.where(lane < HEAD_DIM, scales[0][s], scales[1][s])
            for s in range(SLABS_PER_KV)]


def _attn_prompt_kernel(x_ref, nm_ref, wq_ref, rope_ref, kvc_ref, kvp_ref, sinks_ref, wo_ref,
                        o_ref, q_scr, kx, vx, a_scr, *, tm, tiles_per_seq):
    first_tile = pl.program_id(0) % tiles_per_seq == 0
    _q_proj(x_ref, nm_ref, wq_ref, rope_ref, q_scr)

    kv = jnp.concatenate([kvp_ref[...], kvc_ref[...]], axis=0)
    for f, fr in enumerate(_head_frames(kv[:, 0:KV_DIM])):
        kx[f] = fr.astype(BF16)
    for f, fr in enumerate(_head_frames(kv[:, KV_DIM:2 * KV_DIM])):
        vx[f] = fr.astype(BF16)

    qi = lax.broadcasted_iota(jnp.int32, (WINDOW, 2 * WINDOW), 0)
    sj = lax.broadcasted_iota(jnp.int32, (WINDOW, 2 * WINDOW), 1)
    band = (sj > qi) & (sj <= qi + WINDOW)

    def q_block(qb, carry):
        r0 = pl.multiple_of(qb * WINDOW, WINDOW)
        lo_lim = jnp.where(jnp.logical_and(first_tile, qb == 0), WINDOW, 0)
        mask = band & (sj >= lo_lim)
        for j in range(N_KV_HEADS):
            first_slab = SLABS_PER_KV * j
            qstack = jnp.concatenate(
                [q_scr[pl.ds(r0, WINDOW), (first_slab + s) * LANES:(first_slab + s + 1) * LANES]
                 for s in range(SLABS_PER_KV)], axis=0)
            keys = pl.ds(r0, 2 * WINDOW)
            outs = _attend(qstack, kx[2 * j, keys, :], kx[2 * j + 1, keys, :], vx[2 * j, keys, :],
                           vx[2 * j + 1, keys, :], mask, sinks_ref, first_slab, WINDOW)
            for s, o in enumerate(outs):
                col = (first_slab + s) * LANES
                a_scr[pl.ds(r0, WINDOW), col:col + LANES] = o.astype(BF16)
        return carry

    lax.fori_loop(0, tm // WINDOW, q_block, 0, unroll=Q_BLOCK_UNROLL)
    o_ref[...] = x_ref[...] + _dot(a_scr[...], wo_ref[...])


def _attn_prompt(x, nm, wq, rope, kv, sinks, wo, *, tm, seq):
    rows = x.shape[0]
    tiles_per_seq = seq // tm
    blocks_per_tile = tm // WINDOW
    cur = pl.BlockSpec((tm, 2 * KV_DIM), lambda i: (i, 0))
    prev = pl.BlockSpec((WINDOW, 2 * KV_DIM), lambda i: (jnp.maximum(i * blocks_per_tile - 1, 0), 0))
    return pl.pallas_call(
        functools.partial(_attn_prompt_kernel, tm=tm, tiles_per_seq=tiles_per_seq),
        grid=(rows // tm,),
        in_specs=[
            _row_spec(tm, D_MODEL), _layer_spec((1, D_MODEL), 1), _layer_spec((D_MODEL, D_MODEL), 0),
            pl.BlockSpec((tm, 2 * LANES), lambda i: (i % tiles_per_seq, 0)),
            cur, prev, _SMEM_SPEC, _layer_spec((D_MODEL, D_MODEL), 0),
        ],
        out_specs=_row_spec(tm, D_MODEL),
        out_shape=jax.ShapeDtypeStruct((rows, D_MODEL), F32),
        scratch_shapes=[
            pltpu.VMEM((tm, D_MODEL), BF16),
            pltpu.VMEM((4, WINDOW + tm, LANES), BF16),
            pltpu.VMEM((4, WINDOW + tm, LANES), BF16),
            pltpu.VMEM((tm, D_MODEL), BF16),
        ],
        compiler_params=_params(),
        name="attn_prompt",
    )(x, nm, wq, rope, kv, kv, sinks, wo)


def _attn_sample_kernel(x_ref, nm_ref, wq_ref, rope_ref, ck_ref, cv_ref, kvn_ref, sinks_ref, wo_ref,
                        o_ref, q_scr, a_scr, *, n_seq, n_new):
    _q_proj(x_ref, nm_ref, wq_ref, rope_ref, q_scr)

    rows = SEQ_GROUP * n_new
    n_cached = SEQ_GROUP * WINDOW
    frame = n_cached + LANES
    r = lax.broadcasted_iota(jnp.int32, (rows, frame), 0)
    c = lax.broadcasted_iota(jnp.int32, (rows, frame), 1)
    t_bits, w_bits = n_new.bit_length() - 1, WINDOW.bit_length() - 1
    q_seq, q_t = r >> t_bits, r & (n_new - 1)
    fresh = c - n_cached
    seen_cached = ((c >> w_bits) == q_seq) & ((c & (WINDOW - 1)) > q_t)
    seen_fresh = ((fresh >> t_bits) == q_seq) & ((fresh & (n_new - 1)) <= q_t)
    mask = seen_cached | seen_fresh
    fresh_pad = jnp.zeros((LANES - rows, KV_DIM), F32)

    def seq_group(g, carry):
        r0 = pl.multiple_of(g * rows, rows)
        seqs = pl.ds(g * SEQ_GROUP, SEQ_GROUP)
        kv_new = kvn_ref[pl.ds(r0, rows), :]
        k_fr = [f.astype(BF16) for f in _head_frames(jnp.concatenate(
            [ck_ref[seqs].reshape(n_cached, KV_DIM), kv_new[:, 0:KV_DIM], fresh_pad], axis=0))]
        v_fr = [f.astype(BF16) for f in _head_frames(jnp.concatenate(
            [cv_ref[seqs].reshape(n_cached, KV_DIM), kv_new[:, KV_DIM:2 * KV_DIM], fresh_pad], axis=0))]
        for j in range(N_KV_HEADS):
            first_slab = SLABS_PER_KV * j
            qstack = jnp.concatenate(
                [q_scr[pl.ds(r0, rows), (first_slab + s) * LANES:(first_slab + s + 1) * LANES]
                 for s in range(SLABS_PER_KV)], axis=0)
            outs = _attend(qstack, k_fr[2 * j], k_fr[2 * j + 1], v_fr[2 * j], v_fr[2 * j + 1], mask, sinks_ref,
                           first_slab, rows)
            for s, o in enumerate(outs):
                col = (first_slab + s) * LANES
                a_scr[pl.ds(r0, rows), col:col + LANES] = o.astype(BF16)
        return carry

    lax.fori_loop(0, n_seq // SEQ_GROUP, seq_group, 0)
    o_ref[...] = x_ref[...] + _dot(a_scr[...], wo_ref[...])


def _attn_sample(x, nm, wq, rope, ck, cv, kvn, sinks, wo, *, n_seq, n_new):
    rows = x.shape[0]
    full = lambda shape: pl.BlockSpec(shape, lambda i: (0,) * len(shape))
    return pl.pallas_call(
        functools.partial(_attn_sample_kernel, n_seq=n_seq, n_new=n_new),
        grid=(1,),
        in_specs=[
            full((rows, D_MODEL)), _layer_spec((1, D_MODEL), 1), _layer_spec((D_MODEL, D_MODEL), 0),
            full((rows, 2 * LANES)), full(ck.shape), full(cv.shape), full(kvn.shape),
            _SMEM_SPEC, _layer_spec((D_MODEL, D_MODEL), 0),
        ],
        out_specs=full((rows, D_MODEL)),
        out_shape=jax.ShapeDtypeStruct((rows, D_MODEL), F32),
        scratch_shapes=[pltpu.VMEM((rows, D_MODEL), BF16), pltpu.VMEM((rows, D_MODEL), BF16)],
        compiler_params=_params(),
        name="attn_sample",
    )(x, nm, wq, rope, ck, cv, kvn, sinks, wo)


def _rope_table(pos):
    dim = np.arange(LANES) % HEAD_DIM
    inv_freq = ROPE_THETA ** (-jnp.arange(ROT_HALF, dtype=F32) / ROT_HALF)
    ang = pos.astype(F32)[:, None] * inv_freq[dim % ROT_HALF][None, :]
    in_rot = jnp.asarray(dim < ROT_DIM)[None, :]
    first = jnp.asarray(dim < ROT_HALF)[None, :]
    sin = jnp.sin(ang)
    c = jnp.where(in_rot, jnp.cos(ang), 1.0)
    s = jnp.where(first, -sin, jnp.where(in_rot, sin, 0.0))
    return jnp.concatenate([c, s], axis=1)


def kernel(x_prompt, x_sample, state_ffn_conv, cache_k_win, cache_v_win, p_prompt, p_sample, norm_mix, gm_w_in,
           gm_v_norm, gm_w_s, gm_b_s, gm_w_out, kv_norm, w_kv, w_q, attn_sinks, w_o, norm_ffn, ffn_w_gate,
           ffn_w_up, ffn_conv_w, ffn_conv_b, ffn_w_down, ple_norm, ple_w_gate, ple_w_proj, final_norm):
    batch, seq, _ = x_prompt.shape
    dec_batch, dec_seq, _ = x_sample.shape
    depth = norm_mix.shape[0]
    assert depth == 2 and gm_w_in.shape[0] == 1 and w_q.shape[0] == 1
    tm, tm_ffn = 1024, 512
    assert seq % tm == 0 and seq % tm_ffn == 0 and dec_batch % SEQ_GROUP == 0
    assert dec_seq & (dec_seq - 1) == 0 and (SEQ_GROUP * dec_seq) % (2 * SUBLANES) == 0
    assert SEQ_GROUP * dec_seq <= LANES

    rows3 = lambda a: a.reshape(a.shape[0], 1, a.shape[1])
    ffn_w = dict(
        nf=rows3(norm_ffn), wg=ffn_w_gate.astype(BF16), wu=ffn_w_up.astype(BF16), cw=ffn_conv_w,
        cb=rows3(ffn_conv_b), wd=ffn_w_down.astype(BF16), pn=rows3(ple_norm), pg=ple_w_gate.astype(BF16),
        pp=ple_w_proj.astype(BF16))
    w_in = gm_w_in.astype(BF16)
    w_out = gm_w_out.astype(BF16)
    wq = w_q.astype(BF16)
    wo = w_o.astype(BF16)
    wkv = w_kv.astype(BF16)
    kvn = kv_norm.reshape(1, D_MODEL)
    nm = rows3(norm_mix)
    vn = rows3(gm_v_norm)
    fn = final_norm.reshape(1, D_MODEL)
    sinks = attn_sinks[0]

    n_rows_p = batch * seq
    rope_p = _rope_table(jnp.arange(seq, dtype=jnp.int32))
    bias_p = jnp.repeat(gm_b_s[0].T, GM_GROUP_DIM, axis=1)
    xp = x_prompt.reshape(n_rows_p, D_MODEL)
    p_p = p_prompt.reshape(depth * n_rows_p, PLE_DIM)
    (xp,) = _gmlp(xp, nm, w_in, vn, w_out, tm=tm, ws=gm_w_s, bias=bias_p)
    xp, conv_p0, kv_p = _ffn(xp, p_p, ffn_w, 0, tm=tm_ffn, seq=seq, kv=(kvn, wkv, rope_p))
    xp = _attn_prompt(xp, nm, wq, rope_p, kv_p, sinks, wo, tm=tm, seq=seq)
    y_p, conv_p1 = _ffn(xp, p_p, ffn_w, 1, tm=tm_ffn, seq=seq, p_offset=n_rows_p // tm_ffn, final_norm=fn)

    y_prompt = y_p.reshape(batch, seq, D_MODEL)
    new_conv_prompt = jnp.stack(
        [c.reshape(batch, SUBLANES, D_FF)[:, SUBLANES - (CONV_W - 1):] for c in (conv_p0, conv_p1)], axis=0)
    kv_tail = kv_p.reshape(batch, seq, 2 * KV_DIM)[:, seq - WINDOW:]
    new_k_prompt = kv_tail[:, :, :KV_DIM].reshape(batch, WINDOW, N_KV_HEADS, HEAD_DIM)
    new_v_prompt = kv_tail[:, :, KV_DIM:].reshape(batch, WINDOW, N_KV_HEADS, HEAD_DIM)

    n_rows = dec_batch * dec_seq
    to_tm = lambda a: jnp.swapaxes(a, 0, 1).reshape(a.shape[0] * a.shape[1], a.shape[2])
    from_tm = lambda a, t: jnp.swapaxes(a.reshape(t, dec_batch, a.shape[-1]), 0, 1)
    pos_s = PAST_LEN + jnp.arange(dec_seq, dtype=jnp.int32)
    rope_s = _rope_table(jnp.repeat(pos_s, dec_batch))
    xs = to_tm(x_sample)
    ws_flat = gm_w_s[0][:, :dec_seq, :dec_seq].reshape(-1)
    bs_flat = gm_b_s[0][:, :dec_seq].reshape(-1)
    xs, gm_v = _gmlp(xs, nm, w_in, vn, w_out, tm=n_rows, ws_flat=ws_flat, bs_flat=bs_flat, nt=dec_seq)
    p_s = jnp.swapaxes(p_sample, 1, 2).reshape(depth * n_rows, PLE_DIM)
    state_s = jnp.swapaxes(state_ffn_conv, 1, 2).reshape(depth * (CONV_W - 1) * dec_batch, D_FF)
    xs, conv_s0, kv_s = _ffn(xs, p_s, ffn_w, 0, tm=n_rows, time_steps=dec_seq, state=state_s,
                             kv=(kvn, wkv, rope_s))

    kv_bm = from_tm(kv_s, dec_seq)
    rope_a = _rope_table(jnp.tile(pos_s, dec_batch))
    xa = _attn_sample(from_tm(xs, dec_seq).reshape(n_rows, D_MODEL), nm, wq, rope_a,
                      cache_k_win.reshape(dec_batch, WINDOW, KV_DIM), cache_v_win.reshape(dec_batch, WINDOW, KV_DIM),
                      kv_bm.reshape(n_rows, 2 * KV_DIM), sinks, wo, n_seq=dec_batch, n_new=dec_seq)
    xs = to_tm(xa.reshape(dec_batch, dec_seq, D_MODEL))
    y_s, conv_s1 = _ffn(xs, p_s, ffn_w, 1, tm=n_rows, time_steps=dec_seq, p_offset=1, state=state_s,
                        final_norm=fn)

    y_sample = from_tm(y_s, dec_seq)
    new_gm_v_sample = from_tm(gm_v, dec_seq)[None]
    new_conv_sample = jnp.stack([from_tm(c, CONV_W - 1) for c in (conv_s0, conv_s1)], axis=0)
    new_k_sample = kv_bm[:, :, :KV_DIM].reshape(dec_batch, dec_seq, N_KV_HEADS, HEAD_DIM)
    new_v_sample = kv_bm[:, :, KV_DIM:].reshape(dec_batch, dec_seq, N_KV_HEADS, HEAD_DIM)

    return (y_prompt, y_sample, new_gm_v_sample, new_conv_prompt, new_conv_sample,
            new_k_prompt, new_v_prompt, new_k_sample, new_v_sample)
```
